```python
import jax, jax.numpy as jnp
from jax import lax
import numpy as np

D_MODEL = 1024
BATCH = 2
SEQ = 16384
DEPTH = 1
DEC_BATCH = 1
DEC_SEQ = 16384
PAST_LEN = 128

GRID_W = 64
EPS = 1e-6
GLA_HEADS = 4
GLA_DK = 64
GLA_DV = 128
GLA_RANK = 16
GLA_GATE_NORM = 16.0
GLA_CHUNK = 64
GLA_QK_WIDTH = GLA_HEADS * GLA_DK
GLA_WIDTH = GLA_HEADS * GLA_DV
ATT_HEADS = 8
ATT_KV_HEADS = 2
ATT_GROUP = ATT_HEADS // ATT_KV_HEADS
ATT_HD = 64
ATT_BLOCK = 128
ROPE_THETA = 10000.0
ATT_WIDTH = ATT_HEADS * ATT_HD
ATT_KV_WIDTH = ATT_KV_HEADS * ATT_HD
MIX_WIDTH = GLA_WIDTH + ATT_WIDTH
IN_SIZES = (GLA_QK_WIDTH, GLA_QK_WIDTH, GLA_WIDTH, GLA_WIDTH, GLA_RANK, GLA_RANK,
            ATT_WIDTH, ATT_KV_WIDTH, ATT_KV_WIDTH)
IN_WIDTH = sum(IN_SIZES)
D_FF = 2816
CONV_W = 3

kernel_name = "hymba_gla_axialgqa_convffn_encoder"


def rmsnorm(x, w):
    xf = x.astype(jnp.float32)
    y = xf * lax.rsqrt(jnp.mean(xf * xf, axis=-1, keepdims=True) + EPS)
    return (y * w.astype(jnp.float32)).astype(x.dtype)


def axial_rope_tables(L):
    rows = L // GRID_W
    row = jnp.repeat(jnp.arange(rows), GRID_W).astype(jnp.float32)
    col = jnp.tile(jnp.arange(GRID_W), rows).astype(jnp.float32)
    half = ATT_HD // 2
    inv = 1.0 / (ROPE_THETA ** (jnp.arange(0, half, 2, dtype=jnp.float32) / half))
    ang_r = row[:, None] * inv
    ang_c = col[:, None] * inv
    return (jnp.cos(ang_r), jnp.sin(ang_r), jnp.cos(ang_c), jnp.sin(ang_c))


def _rotate(x, cos, sin):
    h = x.shape[-1] // 2
    x1, x2 = x[..., :h], x[..., h:]
    return jnp.concatenate([x1 * cos - x2 * sin, x1 * sin + x2 * cos], axis=-1)


def apply_axial_rope(x, tabs):
    cr, sr, cc, sc = tabs
    h = ATT_HD // 2
    return jnp.concatenate([_rotate(x[..., :h], cr, sr), _rotate(x[..., h:], cc, sc)], axis=-1)


def gla_chunked(q, k, v, gk):
    B, H, L, dk = q.shape
    dv = v.shape[-1]
    C = GLA_CHUNK
    N = L // C
    q = q.reshape(B, H, N, C, dk)
    k = k.reshape(B, H, N, C, dk)
    v = v.reshape(B, H, N, C, dv)
    b = jnp.cumsum(gk.reshape(B, H, N, C, dk), axis=3)
    b_last = b[..., -1:, :]
    qt = q * jnp.exp(b)
    kt = k * jnp.exp(-b)
    kend = k * jnp.exp(b_last - b)
    mask = jnp.tril(jnp.ones((C, C), dtype=bool))
    A = jnp.where(mask, jnp.einsum('bhncd,bhnsd->bhncs', qt, kt), 0.0)
    o_intra = jnp.einsum('bhncs,bhnse->bhnce', A, v)
    s_chunk = jnp.einsum('bhncd,bhnce->bhnde', kend, v)
    decay = jnp.exp(b_last[..., 0, :])

    def step(S, inp):
        s_c, d_c = inp
        return S * d_c[..., None] + s_c, S

    _, s_prev = lax.scan(step, jnp.zeros((B, H, dk, dv), jnp.float32),
                         (jnp.moveaxis(s_chunk, 2, 0), jnp.moveaxis(decay, 2, 0)))
    s_prev = jnp.moveaxis(s_prev, 0, 2)
    o_inter = jnp.einsum('bhncd,bhnde->bhnce', qt, s_prev)
    return (o_intra + o_inter).reshape(B, H, L, dv)


def gla_group(q, k, v, g, r_f, r_b, w_gk_f, b_gk_f, w_gk_b, b_gk_b, o_norm):
    B, L, _ = q.shape

    def heads(t, d):
        return t.reshape(B, L, GLA_HEADS, d).transpose(0, 2, 1, 3).astype(jnp.float32)

    gk_f = jax.nn.log_sigmoid((r_f @ w_gk_f + b_gk_f).astype(jnp.float32)) / GLA_GATE_NORM
    gk_b = jax.nn.log_sigmoid((r_b @ w_gk_b + b_gk_b).astype(jnp.float32)) / GLA_GATE_NORM
    qh = heads(q, GLA_DK) * (GLA_DK ** -0.5)
    kh = heads(k, GLA_DK)
    vh = heads(v, GLA_DV)
    gf = heads(gk_f, GLA_DK)
    gb = heads(gk_b, GLA_DK)
    o_fwd = gla_chunked(qh, kh, vh, gf)
    flip = lambda t: jnp.flip(t, axis=2)
    o_bwd = flip(gla_chunked(flip(qh), flip(kh), flip(vh), flip(gb)))
    o = (o_fwd + o_bwd).transpose(0, 2, 1, 3)
    o = rmsnorm(o, o_norm).reshape(B, L, GLA_WIDTH)
    return (o * jax.nn.silu(g.astype(jnp.float32))).astype(q.dtype)


def gqa_group(q, k, v, q_norm, k_norm, rope):
    B, L, _ = q.shape
    q = rmsnorm(q.reshape(B, L, ATT_KV_HEADS, ATT_GROUP, ATT_HD), q_norm).astype(jnp.float32)
    k = rmsnorm(k.reshape(B, L, ATT_KV_HEADS, ATT_HD), k_norm).astype(jnp.float32)
    v = v.reshape(B, L, ATT_KV_HEADS, ATT_HD).transpose(0, 2, 1, 3)
    q = apply_axial_rope(q.transpose(0, 2, 3, 1, 4), rope).astype(v.dtype)
    k = apply_axial_rope(k.transpose(0, 2, 1, 3), rope).astype(v.dtype)
    nb = L // ATT_BLOCK
    qb = q.reshape(B, ATT_KV_HEADS, ATT_GROUP, nb, ATT_BLOCK, ATT_HD).transpose(3, 0, 1, 2, 4, 5)
    scale = ATT_HD ** -0.5

    def one_block(qblk):
        s = jnp.einsum('bhgqd,bhkd->bhgqk', qblk, k).astype(jnp.float32) * scale
        p = jax.nn.softmax(s, axis=-1)
        return jnp.einsum('bhgqk,bhkd->bhgqd', p.astype(v.dtype), v)

    o = lax.map(one_block, qb)
    return o.transpose(1, 0, 4, 2, 3, 5).reshape(B, L, ATT_WIDTH)


def mixer_block(h, w_in, w_gk_f, b_gk_f, w_gk_b, b_gk_b, o_norm, q_norm, k_norm, w_out, rope):
    proj = h @ w_in
    points = [int(p) for p in np.cumsum(IN_SIZES)[:-1]]
    gq, gk, gv, gg, rf, rb, aq, ak, av = jnp.split(proj, points, axis=-1)
    o_gla = gla_group(gq, gk, gv, gg, rf, rb, w_gk_f, b_gk_f, w_gk_b, b_gk_b, o_norm)
    o_att = gqa_group(aq, ak, av, q_norm, k_norm, rope)
    return jnp.concatenate([o_gla, o_att.astype(o_gla.dtype)], axis=-1) @ w_out


def conv_ffn(h, w_up, conv_w, conv_b, w_down):
    u = h @ w_up
    up = jnp.pad(u, ((0, 0), (1, 1), (0, 0)))
    u = up[:, :-2] * conv_w[0] + up[:, 1:-1] * conv_w[1] + up[:, 2:] * conv_w[2] + conv_b
    a, gate = jnp.split(u, 2, axis=-1)
    return (jax.nn.silu(gate) * a) @ w_down


def encoder_trunk(x, ln_mix, w_in, w_gk_fwd, b_gk_fwd, w_gk_bwd, b_gk_bwd, gla_out_norm,
                  q_norm, k_norm, w_out, ln_ffn, w_up, conv_w, conv_b, w_down, ln_final):
    rope = axial_rope_tables(x.shape[1])
    for l in range(DEPTH):
        x = x + mixer_block(rmsnorm(x, ln_mix[l]), w_in[l], w_gk_fwd[l], b_gk_fwd[l],
                            w_gk_bwd[l], b_gk_bwd[l], gla_out_norm[l], q_norm[l], k_norm[l],
                            w_out[l], rope)
        x = x + conv_ffn(rmsnorm(x, ln_ffn[l]), w_up[l], conv_w[l], conv_b[l], w_down[l])
    return rmsnorm(x, ln_final)


def setup_inputs(seed: int = 0) -> dict:
    key = jax.random.key(seed)
    ks = jax.random.split(key, 20)
    f32 = jnp.float32
    nrm = lambda k, shape, s: jax.random.normal(k, shape, f32) * s
    gain = lambda k, shape: 1.0 + 0.02 * jax.random.normal(k, shape, f32)
    return {
        "x_prompt": jax.random.normal(ks[0], (BATCH, SEQ, D_MODEL), f32),
        "x_sample": jax.random.normal(ks[1], (DEC_BATCH, DEC_SEQ, D_MODEL), f32),
        "ln_mix": gain(ks[2], (DEPTH, D_MODEL)),
        "w_in": nrm(ks[3], (DEPTH, D_MODEL, IN_WIDTH), D_MODEL ** -0.5),
        "w_gk_fwd": nrm(ks[4], (DEPTH, GLA_RANK, GLA_QK_WIDTH), GLA_RANK ** -0.5),
        "b_gk_fwd": nrm(ks[5], (DEPTH, GLA_QK_WIDTH), 0.1),
        "w_gk_bwd": nrm(ks[6], (DEPTH, GLA_RANK, GLA_QK_WIDTH), GLA_RANK ** -0.5),
        "b_gk_bwd": nrm(ks[7], (DEPTH, GLA_QK_WIDTH), 0.1),
        "gla_out_norm": gain(ks[8], (DEPTH, GLA_DV)),
        "q_norm": gain(ks[9], (DEPTH, ATT_HD)),
        "k_norm": gain(ks[10], (DEPTH, ATT_HD)),
        "w_out": nrm(ks[11], (DEPTH, MIX_WIDTH, D_MODEL), MIX_WIDTH ** -0.5),
        "ln_ffn": gain(ks[12], (DEPTH, D_MODEL)),
        "w_up": nrm(ks[13], (DEPTH, D_MODEL, 2 * D_FF), D_MODEL ** -0.5),
        "conv_w": nrm(ks[14], (DEPTH, CONV_W, 2 * D_FF), CONV_W ** -0.5),
        "conv_b": nrm(ks[15], (DEPTH, 2 * D_FF), 0.02),
        "w_down": nrm(ks[16], (DEPTH, D_FF, D_MODEL), D_FF ** -0.5),
        "ln_final": gain(ks[17], (D_MODEL,)),
    }


def reference(x_prompt, x_sample, ln_mix, w_in, w_gk_fwd, b_gk_fwd, w_gk_bwd, b_gk_bwd,
              gla_out_norm, q_norm, k_norm, w_out, ln_ffn, w_up, conv_w, conv_b, w_down,
              ln_final):
    y_prompt = encoder_trunk(x_prompt, ln_mix, w_in, w_gk_fwd, b_gk_fwd, w_gk_bwd, b_gk_bwd,
                             gla_out_norm, q_norm, k_norm, w_out, ln_ffn, w_up, conv_w,
                             conv_b, w_down, ln_final)
    y_sample = encoder_trunk(x_sample, ln_mix, w_in, w_gk_fwd, b_gk_fwd, w_gk_bwd, b_gk_bwd,
                             gla_out_norm, q_norm, k_norm, w_out, ln_ffn, w_up, conv_w,
                             conv_b, w_down, ln_final)
    return (y_prompt, y_sample)
```

```python
import functools

import jax
import jax.numpy as jnp
from jax import lax
from jax.experimental import pallas as pl
from jax.experimental.pallas import tpu as pltpu

F32 = jnp.float32
BF16 = jnp.bfloat16

GRID_W = 64
EPS = 1e-6
GLA_HEADS = 4
GLA_DK = 64
GLA_DV = 128
GLA_RANK = 16
GLA_GATE_NORM = 16.0
GLA_CHUNK = 64
GLA_QK_WIDTH = GLA_HEADS * GLA_DK
GLA_WIDTH = GLA_HEADS * GLA_DV
ATT_HEADS = 8
ATT_KV_HEADS = 2
ATT_GROUP = ATT_HEADS // ATT_KV_HEADS
ATT_HD = 64
ROPE_THETA = 10000.0
ATT_WIDTH = ATT_HEADS * ATT_HD
ATT_KV_WIDTH = ATT_KV_HEADS * ATT_HD
IN_SIZES = (GLA_QK_WIDTH, GLA_QK_WIDTH, GLA_WIDTH, GLA_WIDTH, GLA_RANK, GLA_RANK,
            ATT_WIDTH, ATT_KV_WIDTH, ATT_KV_WIDTH)

V7X_LANES = 128
V7X_VMEM_BYTES = 64 * 1024 * 1024
VMEM_LIMIT = V7X_VMEM_BYTES - 8 * 1024 * 1024

ROW_TILE = 512
Q_TILE = 256
FFN_ROW_TILE = 512
FFN_COL_TILE = 256
HALO = 16


def _dot(a, b):
    return jnp.dot(a, b, preferred_element_type=F32)


def _dot_nt(a, b):
    return lax.dot_general(a, b, (((1,), (1,)), ((), ())), preferred_element_type=F32)


def _dot_exact(a, b):
    return jnp.dot(a, b, preferred_element_type=F32, precision=lax.Precision.HIGHEST)


def _rms(x, w):
    return x * lax.rsqrt(jnp.mean(x * x, axis=-1, keepdims=True) + EPS) * w


def _segment_mean(xx, bd):
    hi = xx.astype(BF16)
    lo = (xx - hi.astype(F32)).astype(BF16)
    return _dot(hi, bd) + _dot(lo, bd)


def _rope(xn, cos, sin):
    rows = xn.shape[0]
    lane = lax.broadcasted_iota(jnp.int32, (rows, V7X_LANES), 1)
    first = (lane % 32) < 16
    outs = []
    for s in range(xn.shape[1] // V7X_LANES):
        xs = xn[:, s * V7X_LANES:(s + 1) * V7X_LANES]
        partner = jnp.where(first, pltpu.roll(xs, V7X_LANES - 16, 1), pltpu.roll(xs, 16, 1))
        outs.append(xs * cos + partner * sin)
    return outs[0] if len(outs) == 1 else jnp.concatenate(outs, axis=1)


def _in_proj_kernel(x_ref, lnw_ref, wg_ref, wa_ref, wr_ref, wgate_ref, bgate_ref,
                    qn_ref, kn_ref, bdq_ref, bdk_ref, cos_ref, sin_ref,
                    gq_ref, gk_ref, gv_ref, gg_ref, gf_ref, gb_ref, qT_ref, k_ref, vT_ref):
    hb = _rms(x_ref[0], lnw_ref[...]).astype(BF16)

    pg = _dot(hb, wg_ref[...])
    gq_ref[0] = pg[:, 0:GLA_QK_WIDTH] * (GLA_DK ** -0.5)
    gk_ref[0] = pg[:, GLA_QK_WIDTH:2 * GLA_QK_WIDTH]
    o = 2 * GLA_QK_WIDTH
    gv_ref[0] = pg[:, o:o + GLA_WIDTH]
    gg_ref[0] = pg[:, o + GLA_WIDTH:o + 2 * GLA_WIDTH]

    r = _dot(hb, wr_ref[...])
    z = _dot(r.astype(BF16), wgate_ref[...]) + bgate_ref[...]
    ls = (jnp.minimum(z, 0.0) - jnp.log1p(jnp.exp(-jnp.abs(z)))) * (1.0 / GLA_GATE_NORM)
    gf_ref[0] = ls[:, :GLA_QK_WIDTH]
    gb_ref[0] = ls[:, GLA_QK_WIDTH:]

    pa = _dot(hb, wa_ref[...])
    aq = pa[:, :ATT_WIDTH]
    ak = pa[:, ATT_WIDTH:ATT_WIDTH + ATT_KV_WIDTH]
    av = pa[:, ATT_WIDTH + ATT_KV_WIDTH:]
    cos = cos_ref[...]
    sin = sin_ref[...]

    qn = aq * lax.rsqrt(_segment_mean(aq * aq, bdq_ref[...]) + EPS) * qn_ref[...]
    qr = _rope(qn, cos, sin) * (ATT_HD ** -0.5)
    qT_ref[0] = qr.T.astype(BF16)

    kn = ak * lax.rsqrt(_segment_mean(ak * ak, bdk_ref[...]) + EPS) * kn_ref[...]
    kr = _rope(kn, cos, sin).astype(BF16)
    avT = av.T.astype(BF16)
    for h in range(ATT_KV_HEADS):
        k_ref[0, h] = kr[:, h * ATT_HD:(h + 1) * ATT_HD]
        vT_ref[0, h, 0] = avT[h * ATT_HD:(h + 1) * ATT_HD, :]


def _in_proj(x, lnw, wg, wa, wr, wgate, bgate, qn, kn, bdq, bdk, cos, sin):
    B, L, D = x.shape
    tm = ROW_TILE
    nt = L // tm
    full = lambda a: pl.BlockSpec(a.shape, lambda b, i: (0,) * a.ndim)
    row = lambda w: pl.BlockSpec((1, tm, w), lambda b, i: (b, i, 0))
    out_shape = (
        jax.ShapeDtypeStruct((B, L, GLA_QK_WIDTH), F32),
        jax.ShapeDtypeStruct((B, L, GLA_QK_WIDTH), F32),
        jax.ShapeDtypeStruct((B, L, GLA_WIDTH), F32),
        jax.ShapeDtypeStruct((B, L, GLA_WIDTH), F32),
        jax.ShapeDtypeStruct((B, L, GLA_QK_WIDTH), F32),
        jax.ShapeDtypeStruct((B, L, GLA_QK_WIDTH), F32),
        jax.ShapeDtypeStruct((B, ATT_WIDTH, L), BF16),
        jax.ShapeDtypeStruct((B, ATT_KV_HEADS, L, ATT_HD), BF16),
        jax.ShapeDtypeStruct((B, ATT_KV_HEADS, nt, ATT_HD, tm), BF16),
    )
    out_specs = (
        row(GLA_QK_WIDTH), row(GLA_QK_WIDTH), row(GLA_WIDTH), row(GLA_WIDTH),
        row(GLA_QK_WIDTH), row(GLA_QK_WIDTH),
        pl.BlockSpec((1, ATT_WIDTH, tm), lambda b, i: (b, 0, i)),
        pl.BlockSpec((1, ATT_KV_HEADS, tm, ATT_HD), lambda b, i: (b, 0, i, 0)),
        pl.BlockSpec((1, ATT_KV_HEADS, 1, ATT_HD, tm), lambda b, i: (b, 0, i, 0, 0)),
    )
    tab = pl.BlockSpec((tm, V7X_LANES), lambda b, i: (i, 0))
    in_specs = [row(D), full(lnw), full(wg), full(wa), full(wr), full(wgate), full(bgate),
                full(qn), full(kn), full(bdq), full(bdk), tab, tab]
    return pl.pallas_call(
        _in_proj_kernel,
        grid=(B, nt),
        in_specs=in_specs,
        out_specs=out_specs,
        out_shape=out_shape,
        compiler_params=pltpu.CompilerParams(
            dimension_semantics=("parallel", "parallel"), vmem_limit_bytes=VMEM_LIMIT),
        name="in_proj",
    )(x, lnw, wg, wa, wr, wgate, bgate, qn, kn, bdq, bdk, cos, sin)


def _gla_chunk(q_ref, k_ref, v_ref, g_ref, o_ref, s_ref, c, tri, mask, last_row, head_of_lane, ones):
    C = GLA_CHUNK
    sl = pl.ds(pl.multiple_of(c * C, C), C)
    q = q_ref[0, sl, :]
    k = k_ref[0, sl, :]
    v = v_ref[0, sl, :].astype(BF16)
    g = g_ref[0, sl, :]
    b = _dot_exact(tri, g)
    btot = b[last_row:last_row + 1, :]
    qt = q * jnp.exp(b)
    kt = (k * jnp.exp(-b)).astype(BF16)
    kend = k * jnp.exp(btot - b)
    decay = jnp.exp(_dot_exact(g.T, ones))
    s_prev = s_ref[...]
    s_prev_b = s_prev.astype(BF16)
    outs = []
    for h in range(GLA_HEADS):
        qm = jnp.where(head_of_lane == h, qt, 0.0).astype(BF16)
        a = jnp.where(mask, _dot_nt(qm, kt), 0.0)
        outs.append(_dot(a.astype(BF16), v[:, h * GLA_DV:(h + 1) * GLA_DV]) + _dot(qm, s_prev_b))
    o_ref[0, sl, :] = jnp.concatenate(outs, axis=1)
    p = _dot(kend.T.astype(BF16), v)
    upd = jnp.concatenate(
        [p[h * GLA_DK:(h + 1) * GLA_DK, h * GLA_DV:(h + 1) * GLA_DV] for h in range(GLA_HEADS)], axis=0)
    s_ref[...] = s_prev * decay + upd


def _gla_kernel(qf, kf, vf, gf, qb, kb, vb, gb, of_ref, ob_ref, sf_ref, sb_ref):
    @pl.when(pl.program_id(1) == 0)
    def _():
        sf_ref[...] = jnp.zeros_like(sf_ref)
        sb_ref[...] = jnp.zeros_like(sb_ref)

    C = GLA_CHUNK
    nc = qf.shape[1] // C
    r = lax.broadcasted_iota(jnp.int32, (C, C), 0)
    cidx = lax.broadcasted_iota(jnp.int32, (C, C), 1)
    lower = r >= cidx
    upper = r <= cidx
    tri_f = lower.astype(F32)
    tri_b = upper.astype(F32)
    head_of_lane = lax.broadcasted_iota(jnp.int32, (C, GLA_QK_WIDTH), 1) // GLA_DK
    ones = jnp.ones((C, V7X_LANES), F32)

    def body(c, carry):
        _gla_chunk(qf, kf, vf, gf, of_ref, sf_ref, c, tri_f, lower, C - 1, head_of_lane, ones)
        _gla_chunk(qb, kb, vb, gb, ob_ref, sb_ref, nc - 1 - c, tri_b, upper, 0, head_of_lane, ones)
        return carry

    lax.fori_loop(0, nc, body, 0)


def _gla(gq, gk, gv, gf, gb):
    B, L, _ = gq.shape
    T = ROW_TILE
    nt = L // T
    fwd = lambda w: pl.BlockSpec((1, T, w), lambda b, t: (b, t, 0))
    bwd = lambda w: pl.BlockSpec((1, T, w), lambda b, t: (b, nt - 1 - t, 0))
    qk, dv = GLA_QK_WIDTH, GLA_WIDTH
    return pl.pallas_call(
        _gla_kernel,
        grid=(B, nt),
        in_specs=[fwd(qk), fwd(qk), fwd(dv), fwd(qk), bwd(qk), bwd(qk), bwd(dv), bwd(qk)],
        out_specs=(fwd(dv), bwd(dv)),
        out_shape=(jax.ShapeDtypeStruct((B, L, dv), F32), jax.ShapeDtypeStruct((B, L, dv), F32)),
        scratch_shapes=[pltpu.VMEM((qk, GLA_DV), F32), pltpu.VMEM((qk, GLA_DV), F32)],
        compiler_params=pltpu.CompilerParams(
            dimension_semantics=("parallel", "arbitrary"), vmem_limit_bytes=VMEM_LIMIT),
        name="gla",
    )(gq, gk, gv, gf, gq, gk, gv, gb)


def _attn_kernel(qT_ref, k_ref, vT_ref, o_ref, m_ref, l_ref, acc_ref):
    nk, _, tk = vT_ref.shape[2:]
    G, hd = ATT_GROUP, ATT_HD
    m_ref[...] = jnp.full_like(m_ref, -jnp.inf)
    l_ref[...] = jnp.zeros_like(l_ref)
    acc_ref[...] = jnp.zeros_like(acc_ref)

    def body(j, carry):
        kt = k_ref[0, 0, pl.ds(pl.multiple_of(j * tk, tk), tk), :]
        vt = vT_ref[0, 0, j]
        for g in range(G):
            s = _dot(kt, qT_ref[0, g * hd:(g + 1) * hd, :])
            m_prev = m_ref[g]
            m_new = jnp.maximum(m_prev, jnp.max(s, axis=0, keepdims=True))
            alpha = jnp.exp(m_prev - m_new)
            p = jnp.exp(s - m_new)
            l_ref[g] = alpha * l_ref[g] + jnp.sum(p, axis=0, keepdims=True)
            acc_ref[g] = alpha * acc_ref[g] + _dot(vt, p.astype(BF16))
            m_ref[g] = m_new
        return carry

    lax.fori_loop(0, nk, body, 0)
    outs = [(acc_ref[g] / l_ref[g]).T for g in range(G)]
    o_ref[0] = jnp.concatenate(outs, axis=1).astype(o_ref.dtype)


def _attention(qT, k, vT):
    B, _, L = qT.shape
    tq = Q_TILE
    nk, tk = vT.shape[2], vT.shape[4]
    gw = ATT_GROUP * ATT_HD
    return pl.pallas_call(
        _attn_kernel,
        grid=(B, ATT_KV_HEADS, L // tq),
        in_specs=[
            pl.BlockSpec((1, gw, tq), lambda b, h, i: (b, h, i)),
            pl.BlockSpec((1, 1, L, ATT_HD), lambda b, h, i: (b, h, 0, 0)),
            pl.BlockSpec((1, 1, nk, ATT_HD, tk), lambda b, h, i: (b, h, 0, 0, 0)),
        ],
        out_specs=pl.BlockSpec((1, tq, gw), lambda b, h, i: (b, i, h)),
        out_shape=jax.ShapeDtypeStruct((B, L, ATT_WIDTH), BF16),
        scratch_shapes=[pltpu.VMEM((ATT_GROUP, 1, tq), F32), pltpu.VMEM((ATT_GROUP, 1, tq), F32),
                        pltpu.VMEM((ATT_GROUP, ATT_HD, tq), F32)],
        compiler_params=pltpu.CompilerParams(
            dimension_semantics=("parallel", "parallel", "parallel"), vmem_limit_bytes=VMEM_LIMIT),
        name="attn",
    )(qT, k, vT)


def _out_proj_kernel(of_ref, ob_ref, gg_ref, oa_ref, x_ref, on_ref, wo_ref, lnf_ref, x1_ref, h2_ref):
    o = of_ref[0] + ob_ref[0]
    on = on_ref[...]
    normed = [_rms(o[:, h * GLA_DV:(h + 1) * GLA_DV], on) for h in range(GLA_HEADS)]
    g = gg_ref[0]
    og = (jnp.concatenate(normed, axis=1) * (g * (1.0 / (1.0 + jnp.exp(-g))))).astype(BF16)
    mix = _dot(og, wo_ref[:GLA_WIDTH, :]) + _dot(oa_ref[0], wo_ref[GLA_WIDTH:, :])
    x1 = x_ref[0] + mix
    x1_ref[0] = x1
    h2_ref[0] = _rms(x1, lnf_ref[...]).astype(BF16)


def _out_proj(o_f, o_b, gg, o_att, x, on, wo, lnf):
    B, L, D = x.shape
    tm = ROW_TILE
    full = lambda a: pl.BlockSpec(a.shape, lambda b, i: (0,) * a.ndim)
    row = lambda w: pl.BlockSpec((1, tm, w), lambda b, i: (b, i, 0))
    return pl.pallas_call(
        _out_proj_kernel,
        grid=(B, L // tm),
        in_specs=[row(GLA_WIDTH), row(GLA_WIDTH), row(GLA_WIDTH), row(ATT_WIDTH), row(D),
                  full(on), full(wo), full(lnf)],
        out_specs=(row(D), row(D)),
        out_shape=(jax.ShapeDtypeStruct((B, L, D), F32), jax.ShapeDtypeStruct((B, L, D), BF16)),
        compiler_params=pltpu.CompilerParams(
            dimension_semantics=("parallel", "parallel"), vmem_limit_bytes=VMEM_LIMIT),
        name="out_proj",
    )(o_f, o_b, gg, o_att, x, on, wo, lnf)


def _ffn_kernel(hp_ref, hc_ref, hn_ref, x1_ref, wa_ref, wg_ref, cwa_ref, cwg_ref, cba_ref, cbg_ref,
                wd_ref, lnw_ref, y_ref, lhs_ref, acc_ref):
    i = pl.program_id(1)
    j = pl.program_id(2)
    tm = hc_ref.shape[1]

    @pl.when(j == 0)
    def _():
        lhs_ref[0:HALO, :] = jnp.where(i > 0, hp_ref[0], jnp.zeros_like(hp_ref[0]))
        lhs_ref[HALO:HALO + tm, :] = hc_ref[0]
        lhs_ref[HALO + tm:, :] = jnp.where(i < pl.num_programs(1) - 1, hn_ref[0], jnp.zeros_like(hn_ref[0]))
        acc_ref[...] = jnp.zeros_like(acc_ref)

    lhs = lhs_ref[...]

    def conv(u, cw_ref, cb_ref):
        cw = cw_ref[...]
        return (u[HALO - 1:HALO - 1 + tm] * cw[0:1] + u[HALO:HALO + tm] * cw[1:2]
                + u[HALO + 1:HALO + 1 + tm] * cw[2:3] + cb_ref[...])

    a = conv(_dot(lhs, wa_ref[...]), cwa_ref, cba_ref)
    gate = conv(_dot(lhs, wg_ref[...]), cwg_ref, cbg_ref)
    act = (gate * (1.0 / (1.0 + jnp.exp(-gate))) * a).astype(BF16)
    acc_ref[...] += _dot(act, wd_ref[...])

    @pl.when(j == pl.num_programs(2) - 1)
    def _():
        y_ref[0] = _rms(x1_ref[0] + acc_ref[...], lnw_ref[...])


def _ffn(h2, x1, w_up, conv_w, conv_b, w_down, lnw):
    B, L, D = x1.shape
    dff = w_down.shape[0]
    tm, tf = FFN_ROW_TILE, FFN_COL_TILE
    nf = dff // tf
    nt = L // tm
    hb = tm // HALO
    last_halo = L // HALO - 1
    col_a = lambda rows: pl.BlockSpec((rows, tf), lambda b, i, j: (0, j))
    col_g = lambda rows: pl.BlockSpec((rows, tf), lambda b, i, j: (0, nf + j))
    return pl.pallas_call(
        _ffn_kernel,
        grid=(B, nt, nf),
        in_specs=[
            pl.BlockSpec((1, HALO, D), lambda b, i, j: (b, jnp.maximum(i * hb - 1, 0), 0)),
            pl.BlockSpec((1, tm, D), lambda b, i, j: (b, i, 0)),
            pl.BlockSpec((1, HALO, D), lambda b, i, j: (b, jnp.minimum((i + 1) * hb, last_halo), 0)),
            pl.BlockSpec((1, tm, D), lambda b, i, j: (b, i, 0)),
            col_a(D), col_g(D), col_a(3), col_g(3), col_a(1), col_g(1),
            pl.BlockSpec((tf, D), lambda b, i, j: (j, 0)),
            pl.BlockSpec(lnw.shape, lambda b, i, j: (0, 0)),
        ],
        out_specs=pl.BlockSpec((1, tm, D), lambda b, i, j: (b, i, 0)),
        out_shape=jax.ShapeDtypeStruct((B, L, D), F32),
        scratch_shapes=[pltpu.VMEM((tm + 2 * HALO, D), BF16), pltpu.VMEM((tm, D), F32)],
        compiler_params=pltpu.CompilerParams(
            dimension_semantics=("parallel", "parallel", "arbitrary"), vmem_limit_bytes=VMEM_LIMIT),
        name="ffn",
    )(h2, h2, h2, x1, w_up, w_up, conv_w, conv_w, conv_b, conv_b, w_down, lnw)


def _rope_tables(L):
    rows = L // GRID_W
    row = jnp.repeat(jnp.arange(rows), GRID_W).astype(F32)
    col = jnp.tile(jnp.arange(GRID_W), rows).astype(F32)
    half = ATT_HD // 2
    inv = 1.0 / (ROPE_THETA ** (jnp.arange(0, half, 2, dtype=F32) / half))
    ang_r = row[:, None] * inv
    ang_c = col[:, None] * inv
    cr, sr, cc, sc = jnp.cos(ang_r), jnp.sin(ang_r), jnp.cos(ang_c), jnp.sin(ang_c)
    cos = jnp.concatenate([cr, cr, cc, cc], axis=1)
    sin = jnp.concatenate([-sr, sr, -sc, sc], axis=1)
    reps = V7X_LANES // ATT_HD
    return jnp.tile(cos, (1, reps)), jnp.tile(sin, (1, reps))


def _block_diag_mean(width, seg):
    idx = jnp.arange(width) // seg
    return jnp.where(idx[:, None] == idx[None, :], 1.0 / seg, 0.0).astype(BF16)


def _prepare(ln_mix, w_in, w_gk_fwd, b_gk_fwd, w_gk_bwd, b_gk_bwd, gla_out_norm, q_norm, k_norm,
             w_out, ln_ffn, w_up, conv_w, conv_b, w_down, ln_final, layer):
    pts = [0]
    for s in IN_SIZES:
        pts.append(pts[-1] + s)
    w = w_in[layer]
    piece = lambda a, b: w[:, pts[a]:pts[b]]
    zeros = jnp.zeros((GLA_RANK, GLA_QK_WIDTH), F32)
    wgate = jnp.concatenate([jnp.concatenate([w_gk_fwd[layer], zeros], axis=1),
                             jnp.concatenate([zeros, w_gk_bwd[layer]], axis=1)], axis=0)
    return dict(
        lnw=ln_mix[layer][None, :],
        wg=piece(0, 4).astype(BF16),
        wa=piece(6, 9).astype(BF16),
        wr=piece(4, 6).astype(BF16),
        wgate=wgate.astype(BF16),
        bgate=jnp.concatenate([b_gk_fwd[layer], b_gk_bwd[layer]])[None, :],
        qn=jnp.tile(q_norm[layer], ATT_HEADS)[None, :],
        kn=jnp.tile(k_norm[layer], ATT_KV_HEADS)[None, :],
        bdq=_block_diag_mean(ATT_WIDTH, ATT_HD),
        bdk=_block_diag_mean(ATT_KV_WIDTH, ATT_HD),
        on=gla_out_norm[layer][None, :],
        wo=w_out[layer].astype(BF16),
        lnf=ln_ffn[layer][None, :],
        w_up=w_up[layer].astype(BF16),
        conv_w=conv_w[layer],
        conv_b=conv_b[layer][None, :],
        w_down=w_down[layer].astype(BF16),
    )


def _trunk(x, p, cos, sin, ln_final):
    gq, gk, gv, gg, gf, gb, qT, k, vT = _in_proj(
        x, p["lnw"], p["wg"], p["wa"], p["wr"], p["wgate"], p["bgate"], p["qn"], p["kn"],
        p["bdq"], p["bdk"], cos, sin)
    o_f, o_b = _gla(gq, gk, gv, gf, gb)
    o_att = _attention(qT, k, vT)
    x1, h2 = _out_proj(o_f, o_b, gg, o_att, x, p["on"], p["wo"], p["lnf"])
    return _ffn(h2, x1, p["w_up"], p["conv_w"], p["conv_b"], p["w_down"], ln_final[None, :])


def kernel(x_prompt, x_sample, ln_mix, w_in, w_gk_fwd, b_gk_fwd, w_gk_bwd, b_gk_bwd, gla_out_norm,
           q_norm, k_norm, w_out, ln_ffn, w_up, conv_w, conv_b, w_down, ln_final):
    assert w_in.shape[0] == 1, "the FFN kernel fuses the final RMSNorm: single-layer trunk only"
    p = _prepare(ln_mix, w_in, w_gk_fwd, b_gk_fwd, w_gk_bwd, b_gk_bwd, gla_out_norm, q_norm,
                 k_norm, w_out, ln_ffn, w_up, conv_w, conv_b, w_down, ln_final, 0)
    outs = []
    for x in (x_prompt, x_sample):
        cos, sin = _rope_tables(x.shape[1])
        outs.append(_trunk(x, p, cos, sin, ln_final))
    return tuple(outs)
```

```python
import functools

import jax
import jax.numpy as jnp
from jax import lax
from jax.experimental import pallas as pl
from jax.experimental.pallas import tpu as pltpu

F32 = jnp.float32
BF16 = jnp.bfloat16

GRID_W = 64
EPS = 1e-6
GLA_HEADS = 4
GLA_DK = 64
GLA_DV = 128
GLA_RANK = 16
GLA_GATE_NORM = 16.0
GLA_CHUNK = 64
GLA_QK_WIDTH = GLA_HEADS * GLA_DK
GLA_WIDTH = GLA_HEADS * GLA_DV
ATT_HEADS = 8
ATT_KV_HEADS = 2
ATT_GROUP = ATT_HEADS // ATT_KV_HEADS
ATT_HD = 64
ROPE_THETA = 10000.0
ATT_WIDTH = ATT_HEADS * ATT_HD
ATT_KV_WIDTH = ATT_KV_HEADS * ATT_HD
IN_SIZES = (GLA_QK_WIDTH, GLA_QK_WIDTH, GLA_WIDTH, GLA_WIDTH, GLA_RANK, GLA_RANK,
            ATT_WIDTH, ATT_KV_WIDTH, ATT_KV_WIDTH)

V7X_LANES = 128
V7X_VMEM_BYTES = 64 * 1024 * 1024
VMEM_LIMIT = V7X_VMEM_BYTES - 8 * 1024 * 1024

ROW_TILE = 512
Q_TILE = 256
ATT_STEP = 2048
ATT_CHUNK = 256
ATT_ONES_ROWS = 16
FFN_ROW_TILE = 512
FFN_COL_TILE = 256
HALO = 16
LOG2E = 1.4426950408889634


def _dot(a, b):
    return jnp.dot(a, b, preferred_element_type=F32)


def _dot_nt(a, b):
    return lax.dot_general(a, b, (((1,), (1,)), ((), ())), preferred_element_type=F32)


def _dot_01(a01, x):
    x1 = x.astype(BF16)
    r1 = x - x1.astype(F32)
    x2 = r1.astype(BF16)
    x3 = (r1 - x2.astype(F32)).astype(BF16)
    return _dot(a01, x1) + _dot(a01, x2) + _dot(a01, x3)


def _rms(x, w):
    return x * lax.rsqrt(jnp.mean(x * x, axis=-1, keepdims=True) + EPS) * w


def _segment_mean(xx, bd):
    hi = xx.astype(BF16)
    lo = (xx - hi.astype(F32)).astype(BF16)
    return _dot(hi, bd) + _dot(lo, bd)


def _rope(xn, cos, sin):
    rows = xn.shape[0]
    lane = lax.broadcasted_iota(jnp.int32, (rows, V7X_LANES), 1)
    first = (lane % 32) < 16
    outs = []
    for s in range(xn.shape[1] // V7X_LANES):
        xs = xn[:, s * V7X_LANES:(s + 1) * V7X_LANES]
        partner = jnp.where(first, pltpu.roll(xs, V7X_LANES - 16, 1), pltpu.roll(xs, 16, 1))
        outs.append(xs * cos + partner * sin)
    return outs[0] if len(outs) == 1 else jnp.concatenate(outs, axis=1)


def _in_proj_kernel(x_ref, lnw_ref, wg_ref, wa_ref, wr_ref, wgate_ref, bgate_ref,
                    qn_ref, kn_ref, bdq_ref, bdk_ref, cos_ref, sin_ref,
                    gq_ref, gk_ref, gv_ref, gg_ref, gf_ref, gb_ref, qT_ref, k_ref, vT_ref):
    hb = _rms(x_ref[0], lnw_ref[...]).astype(BF16)

    pg = _dot(hb, wg_ref[...])
    gq_ref[0] = pg[:, 0:GLA_QK_WIDTH] * (GLA_DK ** -0.5)
    gk_ref[0] = pg[:, GLA_QK_WIDTH:2 * GLA_QK_WIDTH]
    o = 2 * GLA_QK_WIDTH
    gv_ref[0] = pg[:, o:o + GLA_WIDTH]
    gg_ref[0] = pg[:, o + GLA_WIDTH:o + 2 * GLA_WIDTH]

    r = _dot(hb, wr_ref[...])
    z = _dot(r.astype(BF16), wgate_ref[...]) + bgate_ref[...]
    ls = (jnp.minimum(z, 0.0) - jnp.log1p(jnp.exp(-jnp.abs(z)))) * (1.0 / GLA_GATE_NORM)
    gf_ref[0] = ls[:, :GLA_QK_WIDTH]
    gb_ref[0] = ls[:, GLA_QK_WIDTH:]

    pa = _dot(hb, wa_ref[...])
    aq = pa[:, :ATT_WIDTH]
    ak = pa[:, ATT_WIDTH:ATT_WIDTH + ATT_KV_WIDTH]
    av = pa[:, ATT_WIDTH + ATT_KV_WIDTH:]
    cos = cos_ref[...]
    sin = sin_ref[...]

    qn = aq * lax.rsqrt(_segment_mean(aq * aq, bdq_ref[...]) + EPS) * qn_ref[...]
    qr = _rope(qn, cos, sin) * (LOG2E * ATT_HD ** -0.5)
    qT_ref[0] = qr.T.astype(BF16)

    kn = ak * lax.rsqrt(_segment_mean(ak * ak, bdk_ref[...]) + EPS) * kn_ref[...]
    kr = _rope(kn, cos, sin).astype(BF16)
    avT = av.T.astype(BF16)
    for h in range(ATT_KV_HEADS):
        k_ref[0, h] = kr[:, h * ATT_HD:(h + 1) * ATT_HD]
        vT_ref[0, h, 0] = avT[h * ATT_HD:(h + 1) * ATT_HD, :]


def _in_proj(x, lnw, wg, wa, wr, wgate, bgate, qn, kn, bdq, bdk, cos, sin):
    B, L, D = x.shape
    tm = ROW_TILE
    nt = L // tm
    full = lambda a: pl.BlockSpec(a.shape, lambda b, i: (0,) * a.ndim)
    row = lambda w: pl.BlockSpec((1, tm, w), lambda b, i: (b, i, 0))
    out_shape = (
        jax.ShapeDtypeStruct((B, L, GLA_QK_WIDTH), F32),
        jax.ShapeDtypeStruct((B, L, GLA_QK_WIDTH), F32),
        jax.ShapeDtypeStruct((B, L, GLA_WIDTH), F32),
        jax.ShapeDtypeStruct((B, L, GLA_WIDTH), F32),
        jax.ShapeDtypeStruct((B, L, GLA_QK_WIDTH), F32),
        jax.ShapeDtypeStruct((B, L, GLA_QK_WIDTH), F32),
        jax.ShapeDtypeStruct((B, ATT_WIDTH, L), BF16),
        jax.ShapeDtypeStruct((B, ATT_KV_HEADS, L, ATT_HD), BF16),
        jax.ShapeDtypeStruct((B, ATT_KV_HEADS, nt, ATT_HD, tm), BF16),
    )
    out_specs = (
        row(GLA_QK_WIDTH), row(GLA_QK_WIDTH), row(GLA_WIDTH), row(GLA_WIDTH),
        row(GLA_QK_WIDTH), row(GLA_QK_WIDTH),
        pl.BlockSpec((1, ATT_WIDTH, tm), lambda b, i: (b, 0, i)),
        pl.BlockSpec((1, ATT_KV_HEADS, tm, ATT_HD), lambda b, i: (b, 0, i, 0)),
        pl.BlockSpec((1, ATT_KV_HEADS, 1, ATT_HD, tm), lambda b, i: (b, 0, i, 0, 0)),
    )
    tab = pl.BlockSpec((tm, V7X_LANES), lambda b, i: (i, 0))
    in_specs = [row(D), full(lnw), full(wg), full(wa), full(wr), full(wgate), full(bgate),
                full(qn), full(kn), full(bdq), full(bdk), tab, tab]
    return pl.pallas_call(
        _in_proj_kernel,
        grid=(B, nt),
        in_specs=in_specs,
        out_specs=out_specs,
        out_shape=out_shape,
        compiler_params=pltpu.CompilerParams(
            dimension_semantics=("parallel", "parallel"), vmem_limit_bytes=VMEM_LIMIT),
        name="in_proj",
    )(x, lnw, wg, wa, wr, wgate, bgate, qn, kn, bdq, bdk, cos, sin)


def _gla_chunk(q_ref, k_ref, v_ref, g_ref, o_ref, s_ref, c, tri, mask, last_row, head_of_lane):
    C = GLA_CHUNK
    sl = pl.ds(pl.multiple_of(c * C, C), C)
    q = q_ref[0, sl, :]
    k = k_ref[0, sl, :]
    v = v_ref[0, sl, :].astype(BF16)
    g = g_ref[0, sl, :]
    b = _dot_01(tri, g)
    btot = b[last_row:last_row + 1, :]
    qt = q * jnp.exp(b)
    kt = (k * jnp.exp(-b)).astype(BF16)
    kend = k * jnp.exp(btot - b)
    decay = jnp.exp(jnp.broadcast_to(btot, (V7X_LANES, GLA_QK_WIDTH)).T)
    s_prev = s_ref[...]
    s_prev_b = s_prev.astype(BF16)
    outs = []
    for h in range(GLA_HEADS):
        qm = jnp.where(head_of_lane == h, qt, 0.0).astype(BF16)
        a = jnp.where(mask, _dot_nt(qm, kt), 0.0)
        outs.append(_dot(a.astype(BF16), v[:, h * GLA_DV:(h + 1) * GLA_DV]) + _dot(qm, s_prev_b))
    o_ref[0, sl, :] = jnp.concatenate(outs, axis=1)
    p = _dot(kend.T.astype(BF16), v)
    upd = jnp.concatenate(
        [p[h * GLA_DK:(h + 1) * GLA_DK, h * GLA_DV:(h + 1) * GLA_DV] for h in range(GLA_HEADS)], axis=0)
    s_ref[...] = s_prev * decay + upd


def _gla_kernel(qf, kf, vf, gf, qb, kb, vb, gb, of_ref, ob_ref, sf_ref, sb_ref):
    @pl.when(pl.program_id(1) == 0)
    def _():
        sf_ref[...] = jnp.zeros_like(sf_ref)
        sb_ref[...] = jnp.zeros_like(sb_ref)

    C = GLA_CHUNK
    nc = qf.shape[1] // C
    r = lax.broadcasted_iota(jnp.int32, (C, C), 0)
    cidx = lax.broadcasted_iota(jnp.int32, (C, C), 1)
    lower = r >= cidx
    upper = r <= cidx
    tri_f = lower.astype(BF16)
    tri_b = upper.astype(BF16)
    head_of_lane = lax.broadcasted_iota(jnp.int32, (C, GLA_QK_WIDTH), 1) // GLA_DK

    def body(c, carry):
        _gla_chunk(qf, kf, vf, gf, of_ref, sf_ref, c, tri_f, lower, C - 1, head_of_lane)
        _gla_chunk(qb, kb, vb, gb, ob_ref, sb_ref, nc - 1 - c, tri_b, upper, 0, head_of_lane)
        return carry

    lax.fori_loop(0, nc, body, 0)


def _gla(gq, gk, gv, gf, gb):
    B, L, _ = gq.shape
    T = ROW_TILE
    nt = L // T
    fwd = lambda w: pl.BlockSpec((1, T, w), lambda b, t: (b, t, 0))
    bwd = lambda w: pl.BlockSpec((1, T, w), lambda b, t: (b, nt - 1 - t, 0))
    qk, dv = GLA_QK_WIDTH, GLA_WIDTH
    return pl.pallas_call(
        _gla_kernel,
        grid=(B, nt),
        in_specs=[fwd(qk), fwd(qk), fwd(dv), fwd(qk), bwd(qk), bwd(qk), bwd(dv), bwd(qk)],
        out_specs=(fwd(dv), bwd(dv)),
        out_shape=(jax.ShapeDtypeStruct((B, L, dv), F32), jax.ShapeDtypeStruct((B, L, dv), F32)),
        scratch_shapes=[pltpu.VMEM((qk, GLA_DV), F32), pltpu.VMEM((qk, GLA_DV), F32)],
        compiler_params=pltpu.CompilerParams(
            dimension_semantics=("parallel", "arbitrary"), vmem_limit_bytes=VMEM_LIMIT),
        name="gla",
    )(gq, gk, gv, gf, gq, gk, gv, gb)


def _attn_kernel(qT_ref, k_ref, vT_ref, o_ref, q_ref, s_ref, mx_ref, m_ref, mprev_ref, acc_ref, *, step_keys):
    nkt, _, tk = vT_ref.shape[2:]
    G, hd = ATT_GROUP, ATT_HD
    tq = o_ref.shape[1]
    ch = ATT_CHUNK
    nstep = nkt * tk // step_keys
    nch = step_keys // ch
    for g in range(G):
        q_ref[:, g * tq:(g + 1) * tq] = qT_ref[0, g * hd:(g + 1) * hd, :]
    ones = jnp.ones((ATT_ONES_ROWS, ch), BF16)

    def scores_to_buffer(row0, c):
        s = _dot(k_ref[0, 0, pl.ds(pl.multiple_of(row0, ch), ch), :], q_ref[...])
        s_ref[c * ch:(c + 1) * ch, :] = s
        cm = jnp.max(s.reshape(ch // 8, 8, s.shape[1]), axis=0)
        mx_ref[...] = cm if c == 0 else jnp.maximum(mx_ref[...], cm)

    for c in range(nch):
        scores_to_buffer(c * ch, c)
    m_ref[...] = jnp.max(mx_ref[...], axis=0, keepdims=True)
    mprev_ref[...] = jnp.full_like(mprev_ref, -jnp.inf)
    acc_ref[...] = jnp.zeros_like(acc_ref)

    def body(t, carry):
        t_next = jnp.minimum(t + 1, nstep - 1)
        for c in range(nch):
            p = jnp.exp2(s_ref[c * ch:(c + 1) * ch, :] - m_ref[...]).astype(BF16)
            scores_to_buffer(t_next * step_keys + c * ch, c)
            key0 = t * step_keys + c * ch
            vt = vT_ref[0, 0, key0 // tk, :, pl.ds(pl.multiple_of(key0 % tk, ch), ch)]
            pv = _dot(jnp.concatenate([vt, ones], axis=0), p)
            if c == 0:
                acc_ref[...] = jnp.exp2(mprev_ref[...] - m_ref[...]) * acc_ref[...] + pv
            else:
                acc_ref[...] += pv
        m_t = m_ref[...]
        mprev_ref[...] = m_t
        m_ref[...] = jnp.maximum(m_t, jnp.max(mx_ref[...], axis=0, keepdims=True))
        return carry

    lax.fori_loop(0, nstep, body, 0)
    o = acc_ref[0:hd, :] / acc_ref[hd:hd + 1, :]
    outs = [o[:, g * tq:(g + 1) * tq].T for g in range(G)]
    o_ref[0] = jnp.concatenate(outs, axis=1).astype(o_ref.dtype)


def _attention(qT, k, vT):
    B, _, L = qT.shape
    tq = Q_TILE
    nk, tk = vT.shape[2], vT.shape[4]
    gw = ATT_GROUP * ATT_HD
    W = ATT_GROUP * tq
    step_keys = min(ATT_STEP, L)
    assert L % step_keys == 0 and step_keys % ATT_CHUNK == 0 and tk % ATT_CHUNK == 0
    return pl.pallas_call(
        functools.partial(_attn_kernel, step_keys=step_keys),
        grid=(B, ATT_KV_HEADS, L // tq),
        in_specs=[
            pl.BlockSpec((1, gw, tq), lambda b, h, i: (b, h, i)),
            pl.BlockSpec((1, 1, L, ATT_HD), lambda b, h, i: (b, h, 0, 0)),
            pl.BlockSpec((1, 1, nk, ATT_HD, tk), lambda b, h, i: (b, h, 0, 0, 0)),
        ],
        out_specs=pl.BlockSpec((1, tq, gw), lambda b, h, i: (b, i, h)),
        out_shape=jax.ShapeDtypeStruct((B, L, ATT_WIDTH), BF16),
        scratch_shapes=[pltpu.VMEM((ATT_HD, W), BF16),
                        pltpu.VMEM((step_keys, W), F32),
                        pltpu.VMEM((8, W), F32),
                        pltpu.VMEM((1, W), F32), pltpu.VMEM((1, W), F32),
                        pltpu.VMEM((ATT_HD + ATT_ONES_ROWS, W), F32)],
        compiler_params=pltpu.CompilerParams(
            dimension_semantics=("parallel", "parallel", "parallel"), vmem_limit_bytes=VMEM_LIMIT),
        name="attn",
    )(qT, k, vT)


def _out_proj_kernel(of_ref, ob_ref, gg_ref, oa_ref, x_ref, on_ref, wo_ref, lnf_ref, x1_ref, h2_ref):
    o = of_ref[0] + ob_ref[0]
    on = on_ref[...]
    normed = [_rms(o[:, h * GLA_DV:(h + 1) * GLA_DV], on) for h in range(GLA_HEADS)]
    g = gg_ref[0]
    og = (jnp.concatenate(normed, axis=1) * (g * (1.0 / (1.0 + jnp.exp(-g))))).astype(BF16)
    mix = _dot(og, wo_ref[:GLA_WIDTH, :]) + _dot(oa_ref[0], wo_ref[GLA_WIDTH:, :])
    x1 = x_ref[0] + mix
    x1_ref[0] = x1
    h2_ref[0] = _rms(x1, lnf_ref[...]).astype(BF16)


def _out_proj(o_f, o_b, gg, o_att, x, on, wo, lnf):
    B, L, D = x.shape
    tm = ROW_TILE
    full = lambda a: pl.BlockSpec(a.shape, lambda b, i: (0,) * a.ndim)
    row = lambda w: pl.BlockSpec((1, tm, w), lambda b, i: (b, i, 0))
    return pl.pallas_call(
        _out_proj_kernel,
        grid=(B, L // tm),
        in_specs=[row(GLA_WIDTH), row(GLA_WIDTH), row(GLA_WIDTH), row(ATT_WIDTH), row(D),
                  full(on), full(wo), full(lnf)],
        out_specs=(row(D), row(D)),
        out_shape=(jax.ShapeDtypeStruct((B, L, D), F32), jax.ShapeDtypeStruct((B, L, D), BF16)),
        compiler_params=pltpu.CompilerParams(
            dimension_semantics=("parallel", "parallel"), vmem_limit_bytes=VMEM_LIMIT),
        name="out_proj",
    )(o_f, o_b, gg, o_att, x, on, wo, lnf)


def _ffn_kernel(hp_ref, hc_ref, hn_ref, x1_ref, wup_ref, cw_ref, cb_ref, wd_ref, lnw_ref, y_ref):
    i = pl.program_id(1)
    tm = hc_ref.shape[1]
    dff = wd_ref.shape[0]
    tf = FFN_COL_TILE
    hp = jnp.where(i > 0, hp_ref[0], jnp.zeros_like(hp_ref[0]))
    hn = jnp.where(i < pl.num_programs(1) - 1, hn_ref[0], jnp.zeros_like(hn_ref[0]))
    lhs = jnp.concatenate([hp, hc_ref[0], hn], axis=0)

    def conv(col0):
        u = _dot(lhs, wup_ref[:, col0:col0 + tf])
        cw = cw_ref[:, col0:col0 + tf]
        return (u[HALO - 1:HALO - 1 + tm] * cw[0:1] + u[HALO:HALO + tm] * cw[1:2]
                + u[HALO + 1:HALO + 1 + tm] * cw[2:3] + cb_ref[:, col0:col0 + tf])

    acc = x1_ref[0]
    for j in range(dff // tf):
        a = conv(j * tf)
        gate = conv(dff + j * tf)
        act = (gate * (1.0 / (1.0 + jnp.exp(-gate))) * a).astype(BF16)
        acc = acc + _dot(act, wd_ref[j * tf:(j + 1) * tf, :])
    y_ref[0] = _rms(acc, lnw_ref[...])


def _ffn(h2, x1, w_up, conv_w, conv_b, w_down, lnw):
    B, L, D = x1.shape
    tm = FFN_ROW_TILE
    hb = tm // HALO
    last_halo = L // HALO - 1
    full = lambda a: pl.BlockSpec(a.shape, lambda b, i: (0,) * a.ndim, pipeline_mode=pl.Buffered(1))
    return pl.pallas_call(
        _ffn_kernel,
        grid=(B, L // tm),
        in_specs=[
            pl.BlockSpec((1, HALO, D), lambda b, i: (b, jnp.maximum(i * hb - 1, 0), 0)),
            pl.BlockSpec((1, tm, D), lambda b, i: (b, i, 0)),
            pl.BlockSpec((1, HALO, D), lambda b, i: (b, jnp.minimum((i + 1) * hb, last_halo), 0)),
            pl.BlockSpec((1, tm, D), lambda b, i: (b, i, 0)),
            full(w_up), full(conv_w), full(conv_b), full(w_down), full(lnw),
        ],
        out_specs=pl.BlockSpec((1, tm, D), lambda b, i: (b, i, 0)),
        out_shape=jax.ShapeDtypeStruct((B, L, D), F32),
        compiler_params=pltpu.CompilerParams(
            dimension_semantics=("parallel", "parallel"), vmem_limit_bytes=VMEM_LIMIT),
        name="ffn",
    )(h2, h2, h2, x1, w_up, conv_w, conv_b, w_down, lnw)


def _rope_tables(L):
    rows = L // GRID_W
    row = jnp.repeat(jnp.arange(rows), GRID_W).astype(F32)
    col = jnp.tile(jnp.arange(GRID_W), rows).astype(F32)
    half = ATT_HD // 2
    inv = 1.0 / (ROPE_THETA ** (jnp.arange(0, half, 2, dtype=F32) / half))
    ang_r = row[:, None] * inv
    ang_c = col[:, None] * inv
    cr, sr, cc, sc = jnp.cos(ang_r), jnp.sin(ang_r), jnp.cos(ang_c), jnp.sin(ang_c)
    cos = jnp.concatenate([cr, cr, cc, cc], axis=1)
    sin = jnp.concatenate([-sr, sr, -sc, sc], axis=1)
    reps = V7X_LANES // ATT_HD
    return jnp.tile(cos, (1, reps)), jnp.tile(sin, (1, reps))


def _block_diag_mean(width, seg):
    idx = jnp.arange(width) // seg
    return jnp.where(idx[:, None] == idx[None, :], 1.0 / seg, 0.0).astype(BF16)


def _prepare(ln_mix, w_in, w_gk_fwd, b_gk_fwd, w_gk_bwd, b_gk_bwd, gla_out_norm, q_norm, k_norm,
             w_out, ln_ffn, w_up, conv_w, conv_b, w_down, ln_final, layer):
    pts = [0]
    for s in IN_SIZES:
        pts.append(pts[-1] + s)
    w = w_in[layer]
    piece = lambda a, b: w[:, pts[a]:pts[b]]
    zeros = jnp.zeros((GLA_RANK, GLA_QK_WIDTH), F32)
    wgate = jnp.concatenate([jnp.concatenate([w_gk_fwd[layer], zeros], axis=1),
                             jnp.concatenate([zeros, w_gk_bwd[layer]], axis=1)], axis=0)
    return dict(
        lnw=ln_mix[layer][None, :],
        wg=piece(0, 4).astype(BF16),
        wa=piece(6, 9).astype(BF16),
        wr=piece(4, 6).astype(BF16),
        wgate=wgate.astype(BF16),
        bgate=jnp.concatenate([b_gk_fwd[layer], b_gk_bwd[layer]])[None, :],
        qn=jnp.tile(q_norm[layer], ATT_HEADS)[None, :],
        kn=jnp.tile(k_norm[layer], ATT_KV_HEADS)[None, :],
        bdq=_block_diag_mean(ATT_WIDTH, ATT_HD),
        bdk=_block_diag_mean(ATT_KV_WIDTH, ATT_HD),
        on=gla_out_norm[layer][None, :],
        wo=w_out[layer].astype(BF16),
        lnf=ln_ffn[layer][None, :],
        w_up=w_up[layer].astype(BF16),
        conv_w=conv_w[layer],
        conv_b=conv_b[layer][None, :],
        w_down=w_down[layer].astype(BF16),
    )


def _trunk(x, p, cos, sin, ln_final):
    gq, gk, gv, gg, gf, gb, qT, k, vT = _in_proj(
        x, p["lnw"], p["wg"], p["wa"], p["wr"], p["wgate"], p["bgate"], p["qn"], p["kn"],
        p["bdq"], p["bdk"], cos, sin)
    o_f, o_b = _gla(gq, gk, gv, gf, gb)
    o_att = _attention(qT, k, vT)
    x1, h2 = _out_proj(o_f, o_b, gg, o_att, x, p["on"], p["wo"], p["lnf"])
    return _ffn(h2, x1, p["w_up"], p["conv_w"], p["conv_b"], p["w_down"], ln_final[None, :])


def kernel(x_prompt, x_sample, ln_mix, w_in, w_gk_fwd, b_gk_fwd, w_gk_bwd, b_gk_bwd, gla_out_norm,
           q_norm, k_norm, w_out, ln_ffn, w_up, conv_w, conv_b, w_down, ln_final):
    assert w_in.shape[0] == 1, "the FFN kernel fuses the final RMSNorm: single-layer trunk only"
    p = _prepare(ln_mix, w_in, w_gk_fwd, b_gk_fwd, w_gk_bwd, b_gk_bwd, gla_out_norm, q_norm,
                 k_norm, w_out, ln_ffn, w_up, conv_w, conv_b, w_down, ln_final, 0)
    outs = []
    for x in (x_prompt, x_sample):
        cos, sin = _rope_tables(x.shape[1])
        outs.append(_trunk(x, p, cos, sin, ln_final))
    return tuple(outs)
```

```python
import functools

import jax
import jax.numpy as jnp
from jax import lax
from jax.experimental import pallas as pl
from jax.experimental.pallas import tpu as pltpu

F32 = jnp.float32
BF16 = jnp.bfloat16

GRID_W = 64
EPS = 1e-6
GLA_HEADS = 4
GLA_DK = 64
GLA_DV = 128
GLA_RANK = 16
GLA_GATE_NORM = 16.0
GLA_CHUNK = 64
GLA_QK_WIDTH = GLA_HEADS * GLA_DK
GLA_WIDTH = GLA_HEADS * GLA_DV
ATT_HEADS = 8
ATT_KV_HEADS = 2
ATT_GROUP = ATT_HEADS // ATT_KV_HEADS
ATT_HD = 64
ROPE_THETA = 10000.0
ATT_WIDTH = ATT_HEADS * ATT_HD
ATT_KV_WIDTH = ATT_KV_HEADS * ATT_HD
IN_SIZES = (GLA_QK_WIDTH, GLA_QK_WIDTH, GLA_WIDTH, GLA_WIDTH, GLA_RANK, GLA_RANK,
            ATT_WIDTH, ATT_KV_WIDTH, ATT_KV_WIDTH)

V7X_LANES = 128
V7X_VMEM_BYTES = 64 * 1024 * 1024
VMEM_LIMIT = V7X_VMEM_BYTES - 8 * 1024 * 1024

ROW_TILE = 512
IN_PROJ_SUBTILES = 4
Q_TILE = 256
ATT_STEP = 4096
ATT_CHUNK = 256
ATT_ONES_ROWS = 16
GLA_CHUNKS_PER_ITER = 4
FFN_ROW_TILE = 512
FFN_COL_TILE = 256
HALO = 16
LOG2E = 1.4426950408889634


def _dot(a, b):
    return jnp.dot(a, b, preferred_element_type=F32)


def _dot_nt(a, b):
    return lax.dot_general(a, b, (((1,), (1,)), ((), ())), preferred_element_type=F32)


def _dot_01(a01, x):
    x1 = x.astype(BF16)
    r1 = x - x1.astype(F32)
    x2 = r1.astype(BF16)
    x3 = (r1 - x2.astype(F32)).astype(BF16)
    return _dot(a01, x1) + _dot(a01, x2) + _dot(a01, x3)


def _rms(x, w):
    return x * lax.rsqrt(jnp.mean(x * x, axis=-1, keepdims=True) + EPS) * w


def _segment_mean(xx, bd):
    hi = xx.astype(BF16)
    lo = (xx - hi.astype(F32)).astype(BF16)
    return _dot(hi, bd) + _dot(lo, bd)


def _rope(xn, cos, sin):
    rows = xn.shape[0]
    lane = lax.broadcasted_iota(jnp.int32, (rows, V7X_LANES), 1)
    first = (lane % 32) < 16
    outs = []
    for s in range(xn.shape[1] // V7X_LANES):
        xs = xn[:, s * V7X_LANES:(s + 1) * V7X_LANES]
        partner = jnp.where(first, pltpu.roll(xs, V7X_LANES - 16, 1), pltpu.roll(xs, 16, 1))
        outs.append(xs * cos + partner * sin)
    return outs[0] if len(outs) == 1 else jnp.concatenate(outs, axis=1)


def _in_proj_kernel(x_ref, lnw_ref, wg_ref, wa_ref, wr_ref, wgate_ref, bgate_ref,
                    qn_ref, kn_ref, bdq_ref, bdk_ref, cos_ref, sin_ref,
                    gq_ref, gk_ref, gv_ref, gg_ref, gf_ref, gb_ref, qT_ref, k_ref, vT_ref):
    tm = x_ref.shape[1]
    sub = tm // IN_PROJ_SUBTILES

    def projections(i):
        rows = slice(i * sub, (i + 1) * sub)
        hb = _rms(x_ref[0, rows, :], lnw_ref[...]).astype(BF16)
        return _dot(hb, wg_ref[...]), _dot(hb, wr_ref[...]), _dot(hb, wa_ref[...])

    def epilogue(i, pg, r, pa):
        rows = slice(i * sub, (i + 1) * sub)
        gq_ref[0, rows, :] = pg[:, 0:GLA_QK_WIDTH] * (GLA_DK ** -0.5)
        gk_ref[0, rows, :] = pg[:, GLA_QK_WIDTH:2 * GLA_QK_WIDTH]
        o = 2 * GLA_QK_WIDTH
        gv_ref[0, rows, :] = pg[:, o:o + GLA_WIDTH].astype(gv_ref.dtype)
        gg_ref[0, rows, :] = pg[:, o + GLA_WIDTH:o + 2 * GLA_WIDTH]

        z = _dot(r.astype(BF16), wgate_ref[...]) + bgate_ref[...]
        ls = (jnp.minimum(z, 0.0) - jnp.log1p(jnp.exp(-jnp.abs(z)))) * (1.0 / GLA_GATE_NORM)
        gf_ref[0, rows, :] = ls[:, :GLA_QK_WIDTH]
        gb_ref[0, rows, :] = ls[:, GLA_QK_WIDTH:]

        aq = pa[:, :ATT_WIDTH]
        ak = pa[:, ATT_WIDTH:ATT_WIDTH + ATT_KV_WIDTH]
        av = pa[:, ATT_WIDTH + ATT_KV_WIDTH:]
        cos = cos_ref[rows, :]
        sin = sin_ref[rows, :]
        qn = aq * lax.rsqrt(_segment_mean(aq * aq, bdq_ref[...]) + EPS) * qn_ref[...]
        qr = _rope(qn, cos, sin) * (LOG2E * ATT_HD ** -0.5)
        qT_ref[0, :, rows] = qr.T.astype(BF16)
        kn = ak * lax.rsqrt(_segment_mean(ak * ak, bdk_ref[...]) + EPS) * kn_ref[...]
        kr = _rope(kn, cos, sin).astype(BF16)
        avT = av.T.astype(BF16)
        for h in range(ATT_KV_HEADS):
            k_ref[0, h, rows, :] = kr[:, h * ATT_HD:(h + 1) * ATT_HD]
            vT_ref[0, h, 0, :, rows] = avT[h * ATT_HD:(h + 1) * ATT_HD, :]

    nxt = projections(0)
    for i in range(IN_PROJ_SUBTILES):
        cur = nxt
        if i + 1 < IN_PROJ_SUBTILES:
            nxt = projections(i + 1)
        epilogue(i, *cur)


def _in_proj(x, lnw, wg, wa, wr, wgate, bgate, qn, kn, bdq, bdk, cos, sin):
    B, L, D = x.shape
    tm = ROW_TILE
    nt = L // tm
    full = lambda a: pl.BlockSpec(a.shape, lambda b, i: (0,) * a.ndim)
    row = lambda w: pl.BlockSpec((1, tm, w), lambda b, i: (b, i, 0))
    out_shape = (
        jax.ShapeDtypeStruct((B, L, GLA_QK_WIDTH), F32),
        jax.ShapeDtypeStruct((B, L, GLA_QK_WIDTH), F32),
        jax.ShapeDtypeStruct((B, L, GLA_WIDTH), BF16),
        jax.ShapeDtypeStruct((B, L, GLA_WIDTH), F32),
        jax.ShapeDtypeStruct((B, L, GLA_QK_WIDTH), F32),
        jax.ShapeDtypeStruct((B, L, GLA_QK_WIDTH), F32),
        jax.ShapeDtypeStruct((B, ATT_WIDTH, L), BF16),
        jax.ShapeDtypeStruct((B, ATT_KV_HEADS, L, ATT_HD), BF16),
        jax.ShapeDtypeStruct((B, ATT_KV_HEADS, nt, ATT_HD, tm), BF16),
    )
    out_specs = (
        row(GLA_QK_WIDTH), row(GLA_QK_WIDTH), row(GLA_WIDTH), row(GLA_WIDTH),
        row(GLA_QK_WIDTH), row(GLA_QK_WIDTH),
        pl.BlockSpec((1, ATT_WIDTH, tm), lambda b, i: (b, 0, i)),
        pl.BlockSpec((1, ATT_KV_HEADS, tm, ATT_HD), lambda b, i: (b, 0, i, 0)),
        pl.BlockSpec((1, ATT_KV_HEADS, 1, ATT_HD, tm), lambda b, i: (b, 0, i, 0, 0)),
    )
    tab = pl.BlockSpec((tm, V7X_LANES), lambda b, i: (i, 0))
    in_specs = [row(D), full(lnw), full(wg), full(wa), full(wr), full(wgate), full(bgate),
                full(qn), full(kn), full(bdq), full(bdk), tab, tab]
    return pl.pallas_call(
        _in_proj_kernel,
        grid=(B, nt),
        in_specs=in_specs,
        out_specs=out_specs,
        out_shape=out_shape,
        compiler_params=pltpu.CompilerParams(
            dimension_semantics=("parallel", "parallel"), vmem_limit_bytes=VMEM_LIMIT),
        name="in_proj",
    )(x, lnw, wg, wa, wr, wgate, bgate, qn, kn, bdq, bdk, cos, sin)


def _gla_chunk_stages(q_ref, k_ref, v_ref, g_ref, o_ref, c, tri, mask, last_row, head_of_lane, state):
    C = GLA_CHUNK
    heads = range(GLA_HEADS)
    sl = pl.ds(pl.multiple_of(c * C, C), C)
    q = q_ref[0, sl, :]
    k = k_ref[0, sl, :]
    v = v_ref[0, sl, :]
    vh = [v[:, h * GLA_DV:(h + 1) * GLA_DV] for h in heads]
    b = _dot_01(tri, g_ref[0, sl, :])
    yield
    btot = b[last_row:last_row + 1, :]
    qt = q * jnp.exp(b)
    kt = (k * jnp.exp(-b)).astype(BF16)
    kend_t = (k * jnp.exp(btot - b)).T.astype(BF16)
    decay = jnp.exp(jnp.broadcast_to(btot, (V7X_LANES, GLA_QK_WIDTH)).T)
    qms = [jnp.where(head_of_lane == h, qt, 0.0).astype(BF16) for h in heads]
    scores = [_dot_nt(qm, kt) for qm in qms]
    upd = [_dot(kend_t[h * GLA_DK:(h + 1) * GLA_DK, :], vh[h]) for h in heads]
    yield
    s_prev = state[0]
    s_prev_b = s_prev.astype(BF16)
    outs = [_dot(jnp.where(mask, scores[h], 0.0).astype(BF16), vh[h]) + _dot(qms[h], s_prev_b) for h in heads]
    state[0] = s_prev * decay + jnp.concatenate(upd, axis=0)
    yield
    o_ref[0, sl, :] = jnp.concatenate(outs, axis=1).astype(o_ref.dtype)
    yield


def _gla_kernel(qf, kf, vf, gf, qb, kb, vb, gb, of_ref, ob_ref, sf_ref, sb_ref):
    @pl.when(pl.program_id(1) == 0)
    def _():
        sf_ref[...] = jnp.zeros_like(sf_ref)
        sb_ref[...] = jnp.zeros_like(sb_ref)

    C = GLA_CHUNK
    nc = qf.shape[1] // C
    r = lax.broadcasted_iota(jnp.int32, (C, C), 0)
    cidx = lax.broadcasted_iota(jnp.int32, (C, C), 1)
    lower = r >= cidx
    upper = r <= cidx
    tri_f = lower.astype(BF16)
    tri_b = upper.astype(BF16)
    head_of_lane = lax.broadcasted_iota(jnp.int32, (C, GLA_QK_WIDTH), 1) // GLA_DK

    def body(it, carry):
        sf = [sf_ref[...]]
        sb = [sb_ref[...]]
        chains = []
        for u in range(GLA_CHUNKS_PER_ITER):
            c = it * GLA_CHUNKS_PER_ITER + u
            chains += [
                _gla_chunk_stages(qf, kf, vf, gf, of_ref, c, tri_f, lower, C - 1, head_of_lane, sf),
                _gla_chunk_stages(qb, kb, vb, gb, ob_ref, nc - 1 - c, tri_b, upper, 0, head_of_lane, sb)]
        for _ in range(4):
            for chain in chains:
                next(chain)
        sf_ref[...] = sf[0]
        sb_ref[...] = sb[0]
        return carry

    lax.fori_loop(0, nc // GLA_CHUNKS_PER_ITER, body, 0)


def _gla(gq, gk, gv, gf, gb):
    B, L, _ = gq.shape
    T = ROW_TILE
    nt = L // T
    fwd = lambda w: pl.BlockSpec((1, T, w), lambda b, t: (b, t, 0))
    bwd = lambda w: pl.BlockSpec((1, T, w), lambda b, t: (b, nt - 1 - t, 0))
    qk, dv = GLA_QK_WIDTH, GLA_WIDTH
    return pl.pallas_call(
        _gla_kernel,
        grid=(B, nt),
        in_specs=[fwd(qk), fwd(qk), fwd(dv), fwd(qk), bwd(qk), bwd(qk), bwd(dv), bwd(qk)],
        out_specs=(fwd(dv), bwd(dv)),
        out_shape=(jax.ShapeDtypeStruct((B, L, dv), BF16), jax.ShapeDtypeStruct((B, L, dv), BF16)),
        scratch_shapes=[pltpu.VMEM((qk, GLA_DV), F32), pltpu.VMEM((qk, GLA_DV), F32)],
        compiler_params=pltpu.CompilerParams(
            dimension_semantics=("parallel", "arbitrary"), vmem_limit_bytes=VMEM_LIMIT),
        name="gla",
    )(gq, gk, gv, gf, gq, gk, gv, gb)


def _attn_kernel(qT_ref, qTn_ref, k_ref, vT_ref, o_ref, q_ref, qn_ref, s_ref, mx_ref, m_ref, mprev_ref, acc_ref,
                 *, step_keys):
    nkt, _, tk = vT_ref.shape[2:]
    G, hd = ATT_GROUP, ATT_HD
    tq = o_ref.shape[1]
    ch = ATT_CHUNK
    nstep = nkt * tk // step_keys
    nch = step_keys // ch
    for g in range(G):
        q_ref[:, g * tq:(g + 1) * tq] = qT_ref[0, g * hd:(g + 1) * hd, :]
        qn_ref[:, g * tq:(g + 1) * tq] = qTn_ref[0, g * hd:(g + 1) * hd, :]
    ones = jnp.ones((ATT_ONES_ROWS, ch), BF16)

    def scores_to_buffer(qsrc_ref, row0, c):
        s = _dot(k_ref[0, 0, pl.ds(pl.multiple_of(row0, ch), ch), :], qsrc_ref[...])
        s_ref[c * ch:(c + 1) * ch, :] = s
        cm = jnp.max(s.reshape(ch // 8, 8, s.shape[1]), axis=0)
        mx_ref[...] = cm if c == 0 else jnp.maximum(mx_ref[...], cm)

    @pl.when(pl.program_id(2) == 0)
    def _():
        for c in range(nch):
            scores_to_buffer(q_ref, c * ch, c)
        m_ref[...] = jnp.max(mx_ref[...], axis=0, keepdims=True)

    mprev_ref[...] = jnp.full_like(mprev_ref, -jnp.inf)
    acc_ref[...] = jnp.zeros_like(acc_ref)

    def step(t, qsrc_ref, next_row0, same_queries):
        for c in range(nch):
            p = jnp.exp2(s_ref[c * ch:(c + 1) * ch, :] - m_ref[...]).astype(BF16)
            scores_to_buffer(qsrc_ref, next_row0 + c * ch, c)
            key0 = t * step_keys + c * ch
            vt = vT_ref[0, 0, key0 // tk, :, pl.ds(pl.multiple_of(key0 % tk, ch), ch)]
            pv = _dot(jnp.concatenate([vt, ones], axis=0), p)
            if c == 0:
                acc_ref[...] = jnp.exp2(mprev_ref[...] - m_ref[...]) * acc_ref[...] + pv
            else:
                acc_ref[...] += pv
        new_max = jnp.max(mx_ref[...], axis=0, keepdims=True)
        if same_queries:
            m_t = m_ref[...]
            mprev_ref[...] = m_t
            m_ref[...] = jnp.maximum(m_t, new_max)
        else:
            m_ref[...] = new_max

    def body(t, carry):
        step(t, q_ref, (t + 1) * step_keys, True)
        return carry

    lax.fori_loop(0, nstep - 1, body, 0)
    step(nstep - 1, qn_ref, 0, False)
    o = acc_ref[0:hd, :] / acc_ref[hd:hd + 1, :]
    outs = [o[:, g * tq:(g + 1) * tq].T for g in range(G)]
    o_ref[0] = jnp.concatenate(outs, axis=1).astype(o_ref.dtype)


def _attention(qT, k, vT):
    B, _, L = qT.shape
    tq = Q_TILE
    nq = L // tq
    nk, tk = vT.shape[2], vT.shape[4]
    gw = ATT_GROUP * ATT_HD
    W = ATT_GROUP * tq
    step_keys = min(ATT_STEP, L)
    assert L % step_keys == 0 and step_keys % ATT_CHUNK == 0 and tk % ATT_CHUNK == 0
    return pl.pallas_call(
        functools.partial(_attn_kernel, step_keys=step_keys),
        grid=(B, ATT_KV_HEADS, nq),
        in_specs=[
            pl.BlockSpec((1, gw, tq), lambda b, h, i: (b, h, i)),
            pl.BlockSpec((1, gw, tq), lambda b, h, i: (b, h, jnp.minimum(i + 1, nq - 1))),
            pl.BlockSpec((1, 1, L, ATT_HD), lambda b, h, i: (b, h, 0, 0)),
            pl.BlockSpec((1, 1, nk, ATT_HD, tk), lambda b, h, i: (b, h, 0, 0, 0)),
        ],
        out_specs=pl.BlockSpec((1, tq, gw), lambda b, h, i: (b, i, h)),
        out_shape=jax.ShapeDtypeStruct((B, L, ATT_WIDTH), BF16),
        scratch_shapes=[pltpu.VMEM((ATT_HD, W), BF16), pltpu.VMEM((ATT_HD, W), BF16),
                        pltpu.VMEM((step_keys, W), F32),
                        pltpu.VMEM((8, W), F32),
                        pltpu.VMEM((1, W), F32), pltpu.VMEM((1, W), F32),
                        pltpu.VMEM((ATT_HD + ATT_ONES_ROWS, W), F32)],
        compiler_params=pltpu.CompilerParams(
            dimension_semantics=("parallel", "parallel", "arbitrary"), vmem_limit_bytes=VMEM_LIMIT),
        name="attn",
    )(qT, qT, k, vT)


def _out_proj_kernel(of_ref, ob_ref, gg_ref, oa_ref, x_ref, on_ref, wo_ref, lnf_ref, x1_ref, h2_ref):
    att = _dot(oa_ref[0], wo_ref[GLA_WIDTH:, :])
    o = of_ref[0].astype(F32) + ob_ref[0].astype(F32)
    on = on_ref[...]
    normed = [_rms(o[:, h * GLA_DV:(h + 1) * GLA_DV], on) for h in range(GLA_HEADS)]
    half_g = 0.5 * gg_ref[0]
    og = (jnp.concatenate(normed, axis=1) * (half_g * (1.0 + jnp.tanh(half_g)))).astype(BF16)
    x1 = x_ref[0] + att + _dot(og, wo_ref[:GLA_WIDTH, :])
    x1_ref[0] = x1
    h2_ref[0] = _rms(x1, lnf_ref[...]).astype(BF16)


def _out_proj(o_f, o_b, gg, o_att, x, on, wo, lnf):
    B, L, D = x.shape
    tm = ROW_TILE
    full = lambda a: pl.BlockSpec(a.shape, lambda b, i: (0,) * a.ndim)
    row = lambda w: pl.BlockSpec((1, tm, w), lambda b, i: (b, i, 0))
    return pl.pallas_call(
        _out_proj_kernel,
        grid=(B, L // tm),
        in_specs=[row(GLA_WIDTH), row(GLA_WIDTH), row(GLA_WIDTH), row(ATT_WIDTH), row(D),
                  full(on), full(wo), full(lnf)],
        out_specs=(row(D), row(D)),
        out_shape=(jax.ShapeDtypeStruct((B, L, D), F32), jax.ShapeDtypeStruct((B, L, D), BF16)),
        compiler_params=pltpu.CompilerParams(
            dimension_semantics=("parallel", "parallel"), vmem_limit_bytes=VMEM_LIMIT),
        name="out_proj",
    )(o_f, o_b, gg, o_att, x, on, wo, lnf)


def _ffn_kernel(hp_ref, hc_ref, hn_ref, x1_ref, wup_ref, cw_ref, cb_ref, wd_ref, lnw_ref, y_ref, act_ref):
    i = pl.program_id(1)
    tm = hc_ref.shape[1]
    dff = wd_ref.shape[0]
    tf = FFN_COL_TILE
    rows = tm + 2 * HALO
    hp = jnp.where(i > 0, hp_ref[0], jnp.zeros_like(hp_ref[0]))
    hn = jnp.where(i < pl.num_programs(1) - 1, hn_ref[0], jnp.zeros_like(hn_ref[0]))
    lhs = jnp.concatenate([hp, hc_ref[0], hn], axis=0)

    def up(j):
        return (_dot(lhs, wup_ref[:, j * tf:(j + 1) * tf]),
                _dot(lhs, wup_ref[:, dff + j * tf:dff + (j + 1) * tf]))

    def conv(u, col0):
        cw = cw_ref[:, col0:col0 + tf]
        prev = pltpu.roll(u, 1, 0)[HALO:HALO + tm]
        nxt = pltpu.roll(u, rows - 1, 0)[HALO:HALO + tm]
        return prev * cw[0:1] + u[HALO:HALO + tm] * cw[1:2] + nxt * cw[2:3] + cb_ref[:, col0:col0 + tf]

    n = dff // tf
    u_next = up(0)
    for j in range(n):
        ua, ug = u_next
        if j + 1 < n:
            u_next = up(j + 1)
        a = conv(ua, j * tf)
        half_gate = 0.5 * conv(ug, dff + j * tf)
        act_ref[:, j * tf:(j + 1) * tf] = (half_gate * (1.0 + jnp.tanh(half_gate)) * a).astype(BF16)
    y_ref[0] = _rms(x1_ref[0] + _dot(act_ref[...], wd_ref[...]), lnw_ref[...])


def _ffn(h2, x1, w_up, conv_w, conv_b, w_down, lnw):
    B, L, D = x1.shape
    tm = FFN_ROW_TILE
    hb = tm // HALO
    last_halo = L // HALO - 1
    full = lambda a: pl.BlockSpec(a.shape, lambda b, i: (0,) * a.ndim, pipeline_mode=pl.Buffered(1))
    return pl.pallas_call(
        _ffn_kernel,
        grid=(B, L // tm),
        in_specs=[
            pl.BlockSpec((1, HALO, D), lambda b, i: (b, jnp.maximum(i * hb - 1, 0), 0)),
            pl.BlockSpec((1, tm, D), lambda b, i: (b, i, 0)),
            pl.BlockSpec((1, HALO, D), lambda b, i: (b, jnp.minimum((i + 1) * hb, last_halo), 0)),
            pl.BlockSpec((1, tm, D), lambda b, i: (b, i, 0)),
            full(w_up), full(conv_w), full(conv_b), full(w_down), full(lnw),
        ],
        out_specs=pl.BlockSpec((1, tm, D), lambda b, i: (b, i, 0)),
        out_shape=jax.ShapeDtypeStruct((B, L, D), F32),
        scratch_shapes=[pltpu.VMEM((tm, w_down.shape[0]), BF16)],
        compiler_params=pltpu.CompilerParams(
            dimension_semantics=("parallel", "parallel"), vmem_limit_bytes=VMEM_LIMIT),
        name="ffn",
    )(h2, h2, h2, x1, w_up, conv_w, conv_b, w_down, lnw)


def _rope_tables(L):
    rows = L // GRID_W
    row = jnp.repeat(jnp.arange(rows), GRID_W).astype(F32)
    col = jnp.tile(jnp.arange(GRID_W), rows).astype(F32)
    half = ATT_HD // 2
    inv = 1.0 / (ROPE_THETA ** (jnp.arange(0, half, 2, dtype=F32) / half))
    ang_r = row[:, None] * inv
    ang_c = col[:, None] * inv
    cr, sr, cc, sc = jnp.cos(ang_r), jnp.sin(ang_r), jnp.cos(ang_c), jnp.sin(ang_c)
    cos = jnp.concatenate([cr, cr, cc, cc], axis=1)
    sin = jnp.concatenate([-sr, sr, -sc, sc], axis=1)
    reps = V7X_LANES // ATT_HD
    return jnp.tile(cos, (1, reps)), jnp.tile(sin, (1, reps))


def _block_diag_mean(width, seg):
    idx = jnp.arange(width) // seg
    return jnp.where(idx[:, None] == idx[None, :], 1.0 / seg, 0.0).astype(BF16)


def _prepare(ln_mix, w_in, w_gk_fwd, b_gk_fwd, w_gk_bwd, b_gk_bwd, gla_out_norm, q_norm, k_norm,
             w_out, ln_ffn, w_up, conv_w, conv_b, w_down, ln_final, layer):
    pts = [0]
    for s in IN_SIZES:
        pts.append(pts[-1] + s)
    w = w_in[layer]
    piece = lambda a, b: w[:, pts[a]:pts[b]]
    zeros = jnp.zeros((GLA_RANK, GLA_QK_WIDTH), F32)
    wgate = jnp.concatenate([jnp.concatenate([w_gk_fwd[layer], zeros], axis=1),
                             jnp.concatenate([zeros, w_gk_bwd[layer]], axis=1)], axis=0)
    return dict(
        lnw=ln_mix[layer][None, :],
        wg=piece(0, 4).astype(BF16),
        wa=piece(6, 9).astype(BF16),
        wr=piece(4, 6).astype(BF16),
        wgate=wgate.astype(BF16),
        bgate=jnp.concatenate([b_gk_fwd[layer], b_gk_bwd[layer]])[None, :],
        qn=jnp.tile(q_norm[layer], ATT_HEADS)[None, :],
        kn=jnp.tile(k_norm[layer], ATT_KV_HEADS)[None, :],
        bdq=_block_diag_mean(ATT_WIDTH, ATT_HD),
        bdk=_block_diag_mean(ATT_KV_WIDTH, ATT_HD),
        on=gla_out_norm[layer][None, :],
        wo=w_out[layer].astype(BF16),
        lnf=ln_ffn[layer][None, :],
        w_up=w_up[layer].astype(BF16),
        conv_w=conv_w[layer],
        conv_b=conv_b[layer][None, :],
        w_down=w_down[layer].astype(BF16),
    )


def _trunk(x, p, cos, sin, ln_final):
    gq, gk, gv, gg, gf, gb, qT, k, vT = _in_proj(
        x, p["lnw"], p["wg"], p["wa"], p["wr"], p["wgate"], p["bgate"], p["qn"], p["kn"],
        p["bdq"], p["bdk"], cos, sin)
    o_f, o_b = _gla(gq, gk, gv, gf, gb)
    o_att = _attention(qT, k, vT)
    x1, h2 = _out_proj(o_f, o_b, gg, o_att, x, p["on"], p["wo"], p["lnf"])
    return _ffn(h2, x1, p["w_up"], p["conv_w"], p["conv_b"], p["w_down"], ln_final[None, :])


def kernel(x_prompt, x_sample, ln_mix, w_in, w_gk_fwd, b_gk_fwd, w_gk_bwd, b_gk_bwd, gla_out_norm,
           q_norm, k_norm, w_out, ln_ffn, w_up, conv_w, conv_b, w_down, ln_final):
    assert w_in.shape[0] == 1, "the FFN kernel fuses the final RMSNorm: single-layer trunk only"
    p = _prepare(ln_mix, w_in, w_gk_fwd, b_gk_fwd, w_gk_bwd, b_gk_bwd, gla_out_norm, q_norm,
                 k_norm, w_out, ln_ffn, w_up, conv_w, conv_b, w_down, ln_final, 0)
    outs = []
    for x in (x_prompt, x_sample):
        cos, sin = _rope_tables(x.shape[1])
        outs.append(_trunk(x, p, cos, sin, ln_final))
    return tuple(outs)
```

```python
import functools

import jax
import jax.numpy as jnp
from jax import lax
from jax.experimental import pallas as pl
from jax.experimental.pallas import tpu as pltpu

F32 = jnp.float32
BF16 = jnp.bfloat16

GRID_W = 64
EPS = 1e-6
GLA_HEADS = 4
GLA_DK = 64
GLA_DV = 128
GLA_RANK = 16
GLA_GATE_NORM = 16.0
GLA_CHUNK = 64
GLA_QK_WIDTH = GLA_HEADS * GLA_DK
GLA_WIDTH = GLA_HEADS * GLA_DV
ATT_HEADS = 8
ATT_KV_HEADS = 2
ATT_GROUP = ATT_HEADS // ATT_KV_HEADS
ATT_HD = 64
ROPE_THETA = 10000.0
ATT_WIDTH = ATT_HEADS * ATT_HD
ATT_KV_WIDTH = ATT_KV_HEADS * ATT_HD
IN_SIZES = (GLA_QK_WIDTH, GLA_QK_WIDTH, GLA_WIDTH, GLA_WIDTH, GLA_RANK, GLA_RANK,
            ATT_WIDTH, ATT_KV_WIDTH, ATT_KV_WIDTH)

V7X_LANES = 128
V7X_SUBLANES = 8
V7X_VMEM_BYTES = 64 * 1024 * 1024
VMEM_LIMIT = V7X_VMEM_BYTES - 8 * 1024 * 1024

ROW_TILE = 512
IN_PROJ_SUBTILES = 4
Q_TILE = 256
ATT_STEP = 4096
ATT_CHUNK = 256
ATT_ONES_ROWS = 16
GLA_CHUNKS_PER_ITER = 4
GLA_CHAIN_STAGES = 4
FFN_ROW_TILE = 512
FFN_COL_TILE = 256
HALO = 16
LOG2E = 1.4426950408889634


def _dot(a, b):
    return jnp.dot(a, b, preferred_element_type=F32)


def _dot_nt(a, b):
    return lax.dot_general(a, b, (((1,), (1,)), ((), ())), preferred_element_type=F32)


def _dot_01(a01, x):
    x1 = x.astype(BF16)
    r1 = x - x1.astype(F32)
    x2 = r1.astype(BF16)
    x3 = (r1 - x2.astype(F32)).astype(BF16)
    return _dot(a01, x1) + _dot(a01, x2) + _dot(a01, x3)


def _rms(x, w):
    return x * lax.rsqrt(jnp.mean(x * x, axis=-1, keepdims=True) + EPS) * w


def _norm_rope_t(xt, gain, cos_t, sin_t):
    xn = xt * lax.rsqrt(jnp.mean(xt * xt, axis=0, keepdims=True) + EPS) * gain
    q4 = ATT_HD // 4
    partner = jnp.concatenate([xn[q4:2 * q4], xn[0:q4], xn[3 * q4:], xn[2 * q4:3 * q4]], axis=0)
    return xn * cos_t + partner * sin_t


def _in_proj_kernel(x_ref, lnw_ref, wg_ref, wa_ref, wr_ref, wgate_ref, bgate_ref,
                    qn_ref, kn_ref, cos_ref, sin_ref,
                    gq_ref, gk_ref, gv_ref, gg_ref, gf_ref, gb_ref, qT_ref, k_ref, vT_ref):
    tm = x_ref.shape[1]
    sub = tm // IN_PROJ_SUBTILES

    def projections(i):
        rows = slice(i * sub, (i + 1) * sub)
        hb = _rms(x_ref[0, rows, :], lnw_ref[...]).astype(BF16)
        return _dot(hb, wg_ref[...]), _dot(hb, wr_ref[...]), _dot(hb, wa_ref[...])

    def epilogue(i, pg, r, pa):
        rows = slice(i * sub, (i + 1) * sub)
        gq_ref[0, rows, :] = pg[:, 0:GLA_QK_WIDTH] * (GLA_DK ** -0.5)
        gk_ref[0, rows, :] = pg[:, GLA_QK_WIDTH:2 * GLA_QK_WIDTH]
        o = 2 * GLA_QK_WIDTH
        gv_ref[0, rows, :] = pg[:, o:o + GLA_WIDTH].astype(gv_ref.dtype)
        gg_ref[0, rows, :] = pg[:, o + GLA_WIDTH:o + 2 * GLA_WIDTH]

        z = _dot(r.astype(BF16), wgate_ref[...]) + bgate_ref[...]
        ls = (jnp.minimum(z, 0.0) - jnp.log1p(jnp.exp(-jnp.abs(z)))) * (1.0 / GLA_GATE_NORM)
        gf_ref[0, rows, :] = ls[:, :GLA_QK_WIDTH]
        gb_ref[0, rows, :] = ls[:, GLA_QK_WIDTH:]

        pat = pa.T
        cos_t = cos_ref[:, rows]
        sin_t = sin_ref[:, rows]
        for h in range(ATT_HEADS):
            qh = _norm_rope_t(pat[h * ATT_HD:(h + 1) * ATT_HD], qn_ref[...], cos_t, sin_t)
            qT_ref[0, h * ATT_HD:(h + 1) * ATT_HD, rows] = (qh * (LOG2E * ATT_HD ** -0.5)).astype(BF16)
        for h in range(ATT_KV_HEADS):
            k0 = ATT_WIDTH + h * ATT_HD
            kh = _norm_rope_t(pat[k0:k0 + ATT_HD], kn_ref[...], cos_t, sin_t)
            k_ref[0, h, rows, :] = kh.T.astype(BF16)
            v0 = ATT_WIDTH + ATT_KV_WIDTH + h * ATT_HD
            vT_ref[0, h, 0, :, rows] = pat[v0:v0 + ATT_HD].astype(BF16)

    nxt = projections(0)
    for i in range(IN_PROJ_SUBTILES):
        cur = nxt
        if i + 1 < IN_PROJ_SUBTILES:
            nxt = projections(i + 1)
        epilogue(i, *cur)


def _in_proj(x, lnw, wg, wa, wr, wgate, bgate, qn, kn, cos, sin):
    B, L, D = x.shape
    tm = ROW_TILE
    nt = L // tm
    full = lambda a: pl.BlockSpec(a.shape, lambda b, i: (0,) * a.ndim)
    row = lambda w: pl.BlockSpec((1, tm, w), lambda b, i: (b, i, 0))
    out_shape = (
        jax.ShapeDtypeStruct((B, L, GLA_QK_WIDTH), F32),
        jax.ShapeDtypeStruct((B, L, GLA_QK_WIDTH), F32),
        jax.ShapeDtypeStruct((B, L, GLA_WIDTH), BF16),
        jax.ShapeDtypeStruct((B, L, GLA_WIDTH), F32),
        jax.ShapeDtypeStruct((B, L, GLA_QK_WIDTH), F32),
        jax.ShapeDtypeStruct((B, L, GLA_QK_WIDTH), F32),
        jax.ShapeDtypeStruct((B, ATT_WIDTH, L), BF16),
        jax.ShapeDtypeStruct((B, ATT_KV_HEADS, L, ATT_HD), BF16),
        jax.ShapeDtypeStruct((B, ATT_KV_HEADS, nt, ATT_HD, tm), BF16),
    )
    out_specs = (
        row(GLA_QK_WIDTH), row(GLA_QK_WIDTH), row(GLA_WIDTH), row(GLA_WIDTH),
        row(GLA_QK_WIDTH), row(GLA_QK_WIDTH),
        pl.BlockSpec((1, ATT_WIDTH, tm), lambda b, i: (b, 0, i)),
        pl.BlockSpec((1, ATT_KV_HEADS, tm, ATT_HD), lambda b, i: (b, 0, i, 0)),
        pl.BlockSpec((1, ATT_KV_HEADS, 1, ATT_HD, tm), lambda b, i: (b, 0, i, 0, 0)),
    )
    tab = pl.BlockSpec((ATT_HD, tm), lambda b, i: (0, i))
    in_specs = [row(D), full(lnw), full(wg), full(wa), full(wr), full(wgate), full(bgate),
                full(qn), full(kn), tab, tab]
    return pl.pallas_call(
        _in_proj_kernel,
        grid=(B, nt),
        in_specs=in_specs,
        out_specs=out_specs,
        out_shape=out_shape,
        compiler_params=pltpu.CompilerParams(
            dimension_semantics=("parallel", "parallel"), vmem_limit_bytes=VMEM_LIMIT),
        name="in_proj",
    )(x, lnw, wg, wa, wr, wgate, bgate, qn, kn, cos, sin)


def _gla_chunk_stages(q_ref, k_ref, v_ref, g_ref, o_ref, c, tri, mask, last_row, head_of_lane, state):
    C = GLA_CHUNK
    heads = range(GLA_HEADS)
    sl = pl.ds(pl.multiple_of(c * C, C), C)
    q = q_ref[0, sl, :]
    k = k_ref[0, sl, :]
    v = v_ref[0, sl, :]
    vh = [v[:, h * GLA_DV:(h + 1) * GLA_DV] for h in heads]
    b = _dot_01(tri, g_ref[0, sl, :])
    yield
    btot = b[last_row:last_row + 1, :]
    qt = q * jnp.exp(b)
    kt = (k * jnp.exp(-b)).astype(BF16)
    kend_t = (k * jnp.exp(btot - b)).T.astype(BF16)
    decay = jnp.exp(jnp.broadcast_to(btot, (V7X_LANES, GLA_QK_WIDTH)).T)
    qms = [jnp.where(head_of_lane == h, qt, 0.0).astype(BF16) for h in heads]
    scores = [_dot_nt(qm, kt) for qm in qms]
    upd = [_dot(kend_t[h * GLA_DK:(h + 1) * GLA_DK, :], vh[h]) for h in heads]
    yield
    s_prev = state[0]
    s_prev_b = s_prev.astype(BF16)
    outs = [_dot(jnp.where(mask, scores[h], 0.0).astype(BF16), vh[h]) + _dot(qms[h], s_prev_b) for h in heads]
    state[0] = s_prev * decay + jnp.concatenate(upd, axis=0)
    yield
    o_ref[0, sl, :] = jnp.concatenate(outs, axis=1).astype(o_ref.dtype)
    yield


def _gla_kernel(qf, kf, vf, gf, qb, kb, vb, gb, of_ref, ob_ref, sf_ref, sb_ref):
    @pl.when(pl.program_id(1) == 0)
    def _():
        sf_ref[...] = jnp.zeros_like(sf_ref)
        sb_ref[...] = jnp.zeros_like(sb_ref)

    C = GLA_CHUNK
    nc = qf.shape[1] // C
    r = lax.broadcasted_iota(jnp.int32, (C, C), 0)
    cidx = lax.broadcasted_iota(jnp.int32, (C, C), 1)
    lower = r >= cidx
    upper = r <= cidx
    tri_f = lower.astype(BF16)
    tri_b = upper.astype(BF16)
    head_of_lane = lax.broadcasted_iota(jnp.int32, (C, GLA_QK_WIDTH), 1) // GLA_DK

    def body(it, carry):
        sf = [sf_ref[...]]
        sb = [sb_ref[...]]
        chains = []
        for u in range(GLA_CHUNKS_PER_ITER):
            c = it * GLA_CHUNKS_PER_ITER + u
            chains += [
                _gla_chunk_stages(qf, kf, vf, gf, of_ref, c, tri_f, lower, C - 1, head_of_lane, sf),
                _gla_chunk_stages(qb, kb, vb, gb, ob_ref, nc - 1 - c, tri_b, upper, 0, head_of_lane, sb)]
        for _ in range(GLA_CHAIN_STAGES):
            for chain in chains:
                next(chain)
        sf_ref[...] = sf[0]
        sb_ref[...] = sb[0]
        return carry

    lax.fori_loop(0, nc // GLA_CHUNKS_PER_ITER, body, 0)


def _gla(gq, gk, gv, gf, gb):
    B, L, _ = gq.shape
    T = ROW_TILE
    nt = L // T
    fwd = lambda w: pl.BlockSpec((1, T, w), lambda b, t: (b, t, 0))
    bwd = lambda w: pl.BlockSpec((1, T, w), lambda b, t: (b, nt - 1 - t, 0))
    qk, dv = GLA_QK_WIDTH, GLA_WIDTH
    return pl.pallas_call(
        _gla_kernel,
        grid=(B, nt),
        in_specs=[fwd(qk), fwd(qk), fwd(dv), fwd(qk), bwd(qk), bwd(qk), bwd(dv), bwd(qk)],
        out_specs=(fwd(dv), bwd(dv)),
        out_shape=(jax.ShapeDtypeStruct((B, L, dv), BF16), jax.ShapeDtypeStruct((B, L, dv), BF16)),
        scratch_shapes=[pltpu.VMEM((qk, GLA_DV), F32), pltpu.VMEM((qk, GLA_DV), F32)],
        compiler_params=pltpu.CompilerParams(
            dimension_semantics=("parallel", "arbitrary"), vmem_limit_bytes=VMEM_LIMIT),
        name="gla",
    )(gq, gk, gv, gf, gq, gk, gv, gb)


def _attn_kernel(qT_ref, qTn_ref, k_ref, vT_ref, o_ref, q_ref, qn_ref, s_ref, mx_ref, m_ref, mprev_ref, acc_ref,
                 *, step_keys):
    nkt, _, tk = vT_ref.shape[2:]
    G, hd = ATT_GROUP, ATT_HD
    tq = o_ref.shape[1]
    ch = ATT_CHUNK
    nstep = nkt * tk // step_keys
    nch = step_keys // ch
    for g in range(G):
        q_ref[:, g * tq:(g + 1) * tq] = qT_ref[0, g * hd:(g + 1) * hd, :]
        qn_ref[:, g * tq:(g + 1) * tq] = qTn_ref[0, g * hd:(g + 1) * hd, :]
    ones = jnp.ones((ATT_ONES_ROWS, ch), BF16)

    def scores_to_buffer(qsrc_ref, row0, c):
        s = _dot(k_ref[0, 0, pl.ds(pl.multiple_of(row0, ch), ch), :], qsrc_ref[...])
        s_ref[c * ch:(c + 1) * ch, :] = s
        cm = jnp.max(s.reshape(ch // V7X_SUBLANES, V7X_SUBLANES, s.shape[1]), axis=0)
        mx_ref[...] = cm if c == 0 else jnp.maximum(mx_ref[...], cm)

    @pl.when(pl.program_id(2) == 0)
    def _():
        for c in range(nch):
            scores_to_buffer(q_ref, c * ch, c)
        m_ref[...] = jnp.max(mx_ref[...], axis=0, keepdims=True)

    mprev_ref[...] = jnp.full_like(mprev_ref, -jnp.inf)
    acc_ref[...] = jnp.zeros_like(acc_ref)

    def step(t, qsrc_ref, next_row0, same_queries):
        for c in range(nch):
            p = jnp.exp2(s_ref[c * ch:(c + 1) * ch, :] - m_ref[...]).astype(BF16)
            scores_to_buffer(qsrc_ref, next_row0 + c * ch, c)
            key0 = t * step_keys + c * ch
            vt = vT_ref[0, 0, key0 // tk, :, pl.ds(pl.multiple_of(key0 % tk, ch), ch)]
            pv = _dot(jnp.concatenate([vt, ones], axis=0), p)
            if c == 0:
                acc_ref[...] = jnp.exp2(mprev_ref[...] - m_ref[...]) * acc_ref[...] + pv
            else:
                acc_ref[...] += pv
        new_max = jnp.max(mx_ref[...], axis=0, keepdims=True)
        if same_queries:
            m_t = m_ref[...]
            mprev_ref[...] = m_t
            m_ref[...] = jnp.maximum(m_t, new_max)
        else:
            m_ref[...] = new_max

    def body(t, carry):
        step(t, q_ref, (t + 1) * step_keys, True)
        return carry

    lax.fori_loop(0, nstep - 1, body, 0)
    step(nstep - 1, qn_ref, 0, False)
    o = acc_ref[0:hd, :] / acc_ref[hd:hd + 1, :]
    outs = [o[:, g * tq:(g + 1) * tq].T for g in range(G)]
    o_ref[0] = jnp.concatenate(outs, axis=1).astype(o_ref.dtype)


def _attention(qT, k, vT):
    B, _, L = qT.shape
    tq = Q_TILE
    nq = L // tq
    nk, tk = vT.shape[2], vT.shape[4]
    gw = ATT_GROUP * ATT_HD
    W = ATT_GROUP * tq
    step_keys = min(ATT_STEP, L)
    assert L % step_keys == 0 and step_keys % ATT_CHUNK == 0 and tk % ATT_CHUNK == 0
    return pl.pallas_call(
        functools.partial(_attn_kernel, step_keys=step_keys),
        grid=(B, ATT_KV_HEADS, nq),
        in_specs=[
            pl.BlockSpec((1, gw, tq), lambda b, h, i: (b, h, i)),
            pl.BlockSpec((1, gw, tq), lambda b, h, i: (b, h, jnp.minimum(i + 1, nq - 1))),
            pl.BlockSpec((1, 1, L, ATT_HD), lambda b, h, i: (b, h, 0, 0)),
            pl.BlockSpec((1, 1, nk, ATT_HD, tk), lambda b, h, i: (b, h, 0, 0, 0)),
        ],
        out_specs=pl.BlockSpec((1, tq, gw), lambda b, h, i: (b, i, h)),
        out_shape=jax.ShapeDtypeStruct((B, L, ATT_WIDTH), BF16),
        scratch_shapes=[pltpu.VMEM((ATT_HD, W), BF16), pltpu.VMEM((ATT_HD, W), BF16),
                        pltpu.VMEM((step_keys, W), F32),
                        pltpu.VMEM((V7X_SUBLANES, W), F32),
                        pltpu.VMEM((1, W), F32), pltpu.VMEM((1, W), F32),
                        pltpu.VMEM((ATT_HD + ATT_ONES_ROWS, W), F32)],
        compiler_params=pltpu.CompilerParams(
            dimension_semantics=("parallel", "parallel", "arbitrary"), vmem_limit_bytes=VMEM_LIMIT),
        name="attn",
    )(qT, qT, k, vT)


def _mix_ffn_kernel(ofp, ofc, ofn, obp, obc, obn, ggp, ggc, ggn, oap, oac, oan, xp, xc, xn,
                    on_ref, wo_ref, lnf_ref, wup_ref, cw_ref, cb_ref, wd_ref, lnw_ref, y_ref, act_ref):
    i = pl.program_id(1)
    tm = xc.shape[1]
    dff = wd_ref.shape[0]
    tf = FFN_COL_TILE
    rows = tm + 2 * HALO
    ext = lambda p, c, n: jnp.concatenate([p[0], c[0], n[0]], axis=0)

    att = _dot(ext(oap, oac, oan), wo_ref[GLA_WIDTH:, :])
    o = ext(ofp, ofc, ofn).astype(F32) + ext(obp, obc, obn).astype(F32)
    on = on_ref[...]
    normed = [_rms(o[:, h * GLA_DV:(h + 1) * GLA_DV], on) for h in range(GLA_HEADS)]
    half_g = 0.5 * ext(ggp, ggc, ggn)
    og = (jnp.concatenate(normed, axis=1) * (half_g * (1.0 + jnp.tanh(half_g)))).astype(BF16)
    x1 = ext(xp, xc, xn) + att + _dot(og, wo_ref[:GLA_WIDTH, :])
    h2 = _rms(x1, lnf_ref[...])
    r = lax.broadcasted_iota(jnp.int32, (rows, 1), 0)
    keep = jnp.logical_and(jnp.logical_or(i > 0, r >= HALO),
                           jnp.logical_or(i < pl.num_programs(1) - 1, r < HALO + tm))
    lhs = jnp.where(keep, h2, 0.0).astype(BF16)

    def up(j):
        return (_dot(lhs, wup_ref[:, j * tf:(j + 1) * tf]),
                _dot(lhs, wup_ref[:, dff + j * tf:dff + (j + 1) * tf]))

    def conv(u, col0):
        cw = cw_ref[:, col0:col0 + tf]
        prev = pltpu.roll(u, 1, 0)[HALO:HALO + tm]
        nxt = pltpu.roll(u, rows - 1, 0)[HALO:HALO + tm]
        return prev * cw[0:1] + u[HALO:HALO + tm] * cw[1:2] + nxt * cw[2:3] + cb_ref[:, col0:col0 + tf]

    n = dff // tf
    u_next = up(0)
    for j in range(n):
        ua, ug = u_next
        if j + 1 < n:
            u_next = up(j + 1)
        a = conv(ua, j * tf)
        half_gate = 0.5 * conv(ug, dff + j * tf)
        act_ref[:, j * tf:(j + 1) * tf] = (half_gate * (1.0 + jnp.tanh(half_gate)) * a).astype(BF16)
    y_ref[0] = _rms(x1[HALO:HALO + tm] + _dot(act_ref[...], wd_ref[...]), lnw_ref[...])


def _mix_ffn(o_f, o_b, gg, o_att, x, on, wo, lnf, w_up, conv_w, conv_b, w_down, lnw):
    B, L, D = x.shape
    tm = FFN_ROW_TILE
    hb = tm // HALO
    last_halo = L // HALO - 1
    full = lambda a: pl.BlockSpec(a.shape, lambda b, i: (0,) * a.ndim, pipeline_mode=pl.Buffered(1))

    def halo_tile_halo(w):
        return [pl.BlockSpec((1, HALO, w), lambda b, i: (b, jnp.maximum(i * hb - 1, 0), 0)),
                pl.BlockSpec((1, tm, w), lambda b, i: (b, i, 0)),
                pl.BlockSpec((1, HALO, w), lambda b, i: (b, jnp.minimum((i + 1) * hb, last_halo), 0))]

    acts = [o_f, o_b, gg, o_att, x]
    weights = [on, wo, lnf, w_up, conv_w, conv_b, w_down, lnw]
    return pl.pallas_call(
        _mix_ffn_kernel,
        grid=(B, L // tm),
        in_specs=sum([halo_tile_halo(a.shape[2]) for a in acts], []) + [full(w) for w in weights],
        out_specs=pl.BlockSpec((1, tm, D), lambda b, i: (b, i, 0)),
        out_shape=jax.ShapeDtypeStruct((B, L, D), F32),
        scratch_shapes=[pltpu.VMEM((tm, w_down.shape[0]), BF16)],
        compiler_params=pltpu.CompilerParams(
            dimension_semantics=("parallel", "parallel"), vmem_limit_bytes=VMEM_LIMIT),
        name="mix_ffn",
    )(*sum([[a, a, a] for a in acts], []), *weights)


def _rope_tables(L):
    rows = L // GRID_W
    row = jnp.repeat(jnp.arange(rows), GRID_W).astype(F32)
    col = jnp.tile(jnp.arange(GRID_W), rows).astype(F32)
    half = ATT_HD // 2
    inv = 1.0 / (ROPE_THETA ** (jnp.arange(0, half, 2, dtype=F32) / half))
    ang_r = inv[:, None] * row
    ang_c = inv[:, None] * col
    cr, sr, cc, sc = jnp.cos(ang_r), jnp.sin(ang_r), jnp.cos(ang_c), jnp.sin(ang_c)
    return jnp.concatenate([cr, cr, cc, cc], axis=0), jnp.concatenate([-sr, sr, -sc, sc], axis=0)


def _prepare(ln_mix, w_in, w_gk_fwd, b_gk_fwd, w_gk_bwd, b_gk_bwd, gla_out_norm, q_norm, k_norm,
             w_out, ln_ffn, w_up, conv_w, conv_b, w_down, ln_final, layer):
    pts = [0]
    for s in IN_SIZES:
        pts.append(pts[-1] + s)
    w = w_in[layer]
    piece = lambda a, b: w[:, pts[a]:pts[b]]
    zeros = jnp.zeros((GLA_RANK, GLA_QK_WIDTH), F32)
    sub = ROW_TILE // IN_PROJ_SUBTILES
    wgate = jnp.concatenate([jnp.concatenate([w_gk_fwd[layer], zeros], axis=1),
                             jnp.concatenate([zeros, w_gk_bwd[layer]], axis=1)], axis=0)
    return dict(
        lnw=ln_mix[layer][None, :],
        wg=piece(0, 4).astype(BF16),
        wa=piece(6, 9).astype(BF16),
        wr=piece(4, 6).astype(BF16),
        wgate=wgate.astype(BF16),
        bgate=jnp.concatenate([b_gk_fwd[layer], b_gk_bwd[layer]])[None, :],
        qn=jnp.broadcast_to(q_norm[layer][:, None], (ATT_HD, sub)),
        kn=jnp.broadcast_to(k_norm[layer][:, None], (ATT_HD, sub)),
        on=gla_out_norm[layer][None, :],
        wo=w_out[layer].astype(BF16),
        lnf=ln_ffn[layer][None, :],
        w_up=w_up[layer].astype(BF16),
        conv_w=conv_w[layer],
        conv_b=conv_b[layer][None, :],
        w_down=w_down[layer].astype(BF16),
    )


def _trunk(x, p, cos, sin, ln_final):
    gq, gk, gv, gg, gf, gb, qT, k, vT = _in_proj(
        x, p["lnw"], p["wg"], p["wa"], p["wr"], p["wgate"], p["bgate"], p["qn"], p["kn"], cos, sin)
    o_f, o_b = _gla(gq, gk, gv, gf, gb)
    o_att = _attention(qT, k, vT)
    return _mix_ffn(o_f, o_b, gg, o_att, x, p["on"], p["wo"], p["lnf"],
                    p["w_up"], p["conv_w"], p["conv_b"], p["w_down"], ln_final[None, :])


def kernel(x_prompt, x_sample, ln_mix, w_in, w_gk_fwd, b_gk_fwd, w_gk_bwd, b_gk_bwd, gla_out_norm,
           q_norm, k_norm, w_out, ln_ffn, w_up, conv_w, conv_b, w_down, ln_final):
    assert w_in.shape[0] == 1, "the mix_ffn kernel fuses the final RMSNorm: single-layer trunk only"
    p = _prepare(ln_mix, w_in, w_gk_fwd, b_gk_fwd, w_gk_bwd, b_gk_bwd, gla_out_norm, q_norm,
                 k_norm, w_out, ln_ffn, w_up, conv_w, conv_b, w_down, ln_final, 0)
    outs = []
    for x in (x_prompt, x_sample):
        cos, sin = _rope_tables(x.shape[1])
        outs.append(_trunk(x, p, cos, sin, ln_final))
    return tuple(outs)
```

```python
import functools

import jax
import jax.numpy as jnp
from jax import lax
from jax.experimental import pallas as pl
from jax.experimental.pallas import tpu as pltpu

F32 = jnp.float32
BF16 = jnp.bfloat16

GRID_W = 64
EPS = 1e-6
GLA_HEADS = 4
GLA_DK = 64
GLA_DV = 128
GLA_RANK = 16
GLA_GATE_NORM = 16.0
GLA_CHUNK = 64
GLA_QK_WIDTH = GLA_HEADS * GLA_DK
GLA_WIDTH = GLA_HEADS * GLA_DV
ATT_HEADS = 8
ATT_KV_HEADS = 2
ATT_GROUP = ATT_HEADS // ATT_KV_HEADS
ATT_HD = 64
ROPE_THETA = 10000.0
ATT_WIDTH = ATT_HEADS * ATT_HD
ATT_KV_WIDTH = ATT_KV_HEADS * ATT_HD
IN_SIZES = (GLA_QK_WIDTH, GLA_QK_WIDTH, GLA_WIDTH, GLA_WIDTH, GLA_RANK, GLA_RANK,
            ATT_WIDTH, ATT_KV_WIDTH, ATT_KV_WIDTH)

V7X_LANES = 128
V7X_SUBLANES = 8
V7X_VMEM_BYTES = 64 * 1024 * 1024
VMEM_LIMIT = V7X_VMEM_BYTES - 8 * 1024 * 1024

ROW_TILE = 512
IN_PROJ_SUBTILES = 2
Q_TILE = 128
ATT_STEP = 16384
ATT_CHUNK = 256
ATT_ONES_ROWS = 16
GLA_CHUNKS_PER_ITER = 4
GLA_CHAIN_STAGES = 4
FFN_ROW_TILE = 512
FFN_COL_TILE = 256
HALO = 16
LOG2E = 1.4426950408889634


def _dot(a, b):
    return jnp.dot(a, b, preferred_element_type=F32)


def _dot_nt(a, b):
    return lax.dot_general(a, b, (((1,), (1,)), ((), ())), preferred_element_type=F32)


def _dot_01(a01, x):
    x1 = x.astype(BF16)
    r1 = x - x1.astype(F32)
    x2 = r1.astype(BF16)
    x3 = (r1 - x2.astype(F32)).astype(BF16)
    return _dot(a01, x1) + _dot(a01, x2) + _dot(a01, x3)


def _rms(x, w):
    return x * lax.rsqrt(jnp.mean(x * x, axis=-1, keepdims=True) + EPS) * w


def _norm_rope_t(xt, gain, cos_t, sin_t):
    xn = xt * lax.rsqrt(jnp.mean(xt * xt, axis=0, keepdims=True) + EPS) * gain
    q4 = ATT_HD // 4
    partner = jnp.concatenate([xn[q4:2 * q4], xn[0:q4], xn[3 * q4:], xn[2 * q4:3 * q4]], axis=0)
    return xn * cos_t + partner * sin_t


def _in_proj_kernel(x_ref, lnw_ref, wg_ref, wa_ref, wr_ref, wgate_ref, bgate_ref,
                    qn_ref, kn_ref, cos_ref, sin_ref,
                    gq_ref, gk_ref, gv_ref, gg_ref, gf_ref, gb_ref, qT_ref, k_ref, vT_ref):
    tm = x_ref.shape[1]
    sub = tm // IN_PROJ_SUBTILES

    def projections(i):
        rows = slice(i * sub, (i + 1) * sub)
        hb = _rms(x_ref[0, rows, :], lnw_ref[...]).astype(BF16)
        return _dot(hb, wg_ref[...]), _dot(hb, wr_ref[...]), _dot(hb, wa_ref[...])

    def epilogue(i, pg, r, pa):
        rows = slice(i * sub, (i + 1) * sub)
        gq_ref[0, rows, :] = pg[:, 0:GLA_QK_WIDTH] * (GLA_DK ** -0.5)
        gk_ref[0, rows, :] = pg[:, GLA_QK_WIDTH:2 * GLA_QK_WIDTH]
        o = 2 * GLA_QK_WIDTH
        gv_ref[0, rows, :] = pg[:, o:o + GLA_WIDTH].astype(gv_ref.dtype)
        gg_ref[0, rows, :] = pg[:, o + GLA_WIDTH:o + 2 * GLA_WIDTH]

        z = _dot(r.astype(BF16), wgate_ref[...]) + bgate_ref[...]
        ls = (jnp.minimum(z, 0.0) - jnp.log1p(jnp.exp(-jnp.abs(z)))) * (1.0 / GLA_GATE_NORM)
        gf_ref[0, rows, :] = ls[:, :GLA_QK_WIDTH]
        gb_ref[0, rows, :] = ls[:, GLA_QK_WIDTH:]

        pat = pa.T
        cos_t = cos_ref[:, rows]
        sin_t = sin_ref[:, rows]
        for h in range(ATT_HEADS):
            qh = _norm_rope_t(pat[h * ATT_HD:(h + 1) * ATT_HD], qn_ref[...], cos_t, sin_t)
            qT_ref[0, h * ATT_HD:(h + 1) * ATT_HD, rows] = (qh * (LOG2E * ATT_HD ** -0.5)).astype(BF16)
        for h in range(ATT_KV_HEADS):
            k0 = ATT_WIDTH + h * ATT_HD
            kh = _norm_rope_t(pat[k0:k0 + ATT_HD], kn_ref[...], cos_t, sin_t)
            k_ref[0, h, rows, :] = kh.T.astype(BF16)
            v0 = ATT_WIDTH + ATT_KV_WIDTH + h * ATT_HD
            vT_ref[0, h, 0, :, rows] = pat[v0:v0 + ATT_HD].astype(BF16)

    nxt = projections(0)
    for i in range(IN_PROJ_SUBTILES):
        cur = nxt
        if i + 1 < IN_PROJ_SUBTILES:
            nxt = projections(i + 1)
        epilogue(i, *cur)


def _in_proj(x, lnw, wg, wa, wr, wgate, bgate, qn, kn, cos, sin):
    B, L, D = x.shape
    tm = ROW_TILE
    nt = L // tm
    full = lambda a: pl.BlockSpec(a.shape, lambda b, i: (0,) * a.ndim)
    row = lambda w: pl.BlockSpec((1, tm, w), lambda b, i: (b, i, 0))
    out_shape = (
        jax.ShapeDtypeStruct((B, L, GLA_QK_WIDTH), F32),
        jax.ShapeDtypeStruct((B, L, GLA_QK_WIDTH), F32),
        jax.ShapeDtypeStruct((B, L, GLA_WIDTH), BF16),
        jax.ShapeDtypeStruct((B, L, GLA_WIDTH), F32),
        jax.ShapeDtypeStruct((B, L, GLA_QK_WIDTH), F32),
        jax.ShapeDtypeStruct((B, L, GLA_QK_WIDTH), F32),
        jax.ShapeDtypeStruct((B, ATT_WIDTH, L), BF16),
        jax.ShapeDtypeStruct((B, ATT_KV_HEADS, L, ATT_HD), BF16),
        jax.ShapeDtypeStruct((B, ATT_KV_HEADS, nt, ATT_HD, tm), BF16),
    )
    out_specs = (
        row(GLA_QK_WIDTH), row(GLA_QK_WIDTH), row(GLA_WIDTH), row(GLA_WIDTH),
        row(GLA_QK_WIDTH), row(GLA_QK_WIDTH),
        pl.BlockSpec((1, ATT_WIDTH, tm), lambda b, i: (b, 0, i)),
        pl.BlockSpec((1, ATT_KV_HEADS, tm, ATT_HD), lambda b, i: (b, 0, i, 0)),
        pl.BlockSpec((1, ATT_KV_HEADS, 1, ATT_HD, tm), lambda b, i: (b, 0, i, 0, 0)),
    )
    tab = pl.BlockSpec((ATT_HD, tm), lambda b, i: (0, i))
    in_specs = [row(D), full(lnw), full(wg), full(wa), full(wr), full(wgate), full(bgate),
                full(qn), full(kn), tab, tab]
    return pl.pallas_call(
        _in_proj_kernel,
        grid=(B, nt),
        in_specs=in_specs,
        out_specs=out_specs,
        out_shape=out_shape,
        compiler_params=pltpu.CompilerParams(
            dimension_semantics=("parallel", "parallel"), vmem_limit_bytes=VMEM_LIMIT),
        name="in_proj",
    )(x, lnw, wg, wa, wr, wgate, bgate, qn, kn, cos, sin)


def _gla_chunk_stages(q_ref, k_ref, v_ref, g_ref, o_ref, c, tri, mask, last_row, head_of_lane, state):
    C = GLA_CHUNK
    heads = range(GLA_HEADS)
    sl = pl.ds(pl.multiple_of(c * C, C), C)
    q = q_ref[0, sl, :]
    k = k_ref[0, sl, :]
    v = v_ref[0, sl, :]
    vh = [v[:, h * GLA_DV:(h + 1) * GLA_DV] for h in heads]
    b = _dot_01(tri, g_ref[0, sl, :])
    yield
    btot = b[last_row:last_row + 1, :]
    qt = q * jnp.exp(b)
    kt = (k * jnp.exp(-b)).astype(BF16)
    kend_t = (k * jnp.exp(btot - b)).T.astype(BF16)
    decay = jnp.exp(jnp.broadcast_to(btot, (V7X_LANES, GLA_QK_WIDTH)).T)
    qms = [jnp.where(head_of_lane == h, qt, 0.0).astype(BF16) for h in heads]
    scores = [_dot_nt(qm, kt) for qm in qms]
    upd = [_dot(kend_t[h * GLA_DK:(h + 1) * GLA_DK, :], vh[h]) for h in heads]
    yield
    s_prev = state[0]
    s_prev_b = s_prev.astype(BF16)
    outs = [_dot(jnp.where(mask, scores[h], 0.0).astype(BF16), vh[h]) + _dot(qms[h], s_prev_b) for h in heads]
    state[0] = s_prev * decay + jnp.concatenate(upd, axis=0)
    yield
    o_ref[0, sl, :] = jnp.concatenate(outs, axis=1).astype(o_ref.dtype)
    yield


def _gla_kernel(qf, kf, vf, gf, qb, kb, vb, gb, of_ref, ob_ref, sf_ref, sb_ref):
    @pl.when(pl.program_id(1) == 0)
    def _():
        sf_ref[...] = jnp.zeros_like(sf_ref)
        sb_ref[...] = jnp.zeros_like(sb_ref)

    C = GLA_CHUNK
    nc = qf.shape[1] // C
    r = lax.broadcasted_iota(jnp.int32, (C, C), 0)
    cidx = lax.broadcasted_iota(jnp.int32, (C, C), 1)
    lower = r >= cidx
    upper = r <= cidx
    tri_f = lower.astype(BF16)
    tri_b = upper.astype(BF16)
    head_of_lane = lax.broadcasted_iota(jnp.int32, (C, GLA_QK_WIDTH), 1) // GLA_DK

    def body(it, carry):
        sf = [sf_ref[...]]
        sb = [sb_ref[...]]
        chains = []
        for u in range(GLA_CHUNKS_PER_ITER):
            c = it * GLA_CHUNKS_PER_ITER + u
            chains += [
                _gla_chunk_stages(qf, kf, vf, gf, of_ref, c, tri_f, lower, C - 1, head_of_lane, sf),
                _gla_chunk_stages(qb, kb, vb, gb, ob_ref, nc - 1 - c, tri_b, upper, 0, head_of_lane, sb)]
        for _ in range(GLA_CHAIN_STAGES):
            for chain in chains:
                next(chain)
        sf_ref[...] = sf[0]
        sb_ref[...] = sb[0]
        return carry

    lax.fori_loop(0, nc // GLA_CHUNKS_PER_ITER, body, 0)


def _gla(gq, gk, gv, gf, gb):
    B, L, _ = gq.shape
    T = ROW_TILE
    nt = L // T
    fwd = lambda w: pl.BlockSpec((1, T, w), lambda b, t: (b, t, 0))
    bwd = lambda w: pl.BlockSpec((1, T, w), lambda b, t: (b, nt - 1 - t, 0))
    qk, dv = GLA_QK_WIDTH, GLA_WIDTH
    return pl.pallas_call(
        _gla_kernel,
        grid=(B, nt),
        in_specs=[fwd(qk), fwd(qk), fwd(dv), fwd(qk), bwd(qk), bwd(qk), bwd(dv), bwd(qk)],
        out_specs=(fwd(dv), bwd(dv)),
        out_shape=(jax.ShapeDtypeStruct((B, L, dv), BF16), jax.ShapeDtypeStruct((B, L, dv), BF16)),
        scratch_shapes=[pltpu.VMEM((qk, GLA_DV), F32), pltpu.VMEM((qk, GLA_DV), F32)],
        compiler_params=pltpu.CompilerParams(
            dimension_semantics=("parallel", "arbitrary"), vmem_limit_bytes=VMEM_LIMIT),
        name="gla",
    )(gq, gk, gv, gf, gq, gk, gv, gb)


def _attn_kernel(qT_ref, qTn_ref, k_ref, vT_ref, o_ref, q_ref, qn_ref, s_ref, mx_ref, m_ref, mprev_ref, acc_ref,
                 *, step_keys):
    nkt, _, tk = vT_ref.shape[2:]
    G, hd = ATT_GROUP, ATT_HD
    tq = o_ref.shape[1]
    ch = ATT_CHUNK
    nstep = nkt * tk // step_keys
    nch = step_keys // ch
    for g in range(G):
        q_ref[:, g * tq:(g + 1) * tq] = qT_ref[0, g * hd:(g + 1) * hd, :]
        qn_ref[:, g * tq:(g + 1) * tq] = qTn_ref[0, g * hd:(g + 1) * hd, :]
    ones = jnp.ones((ATT_ONES_ROWS, ch), BF16)

    def scores_to_buffer(qsrc_ref, row0, c):
        s = _dot(k_ref[0, 0, pl.ds(pl.multiple_of(row0, ch), ch), :], qsrc_ref[...])
        s_ref[c * ch:(c + 1) * ch, :] = s
        cm = jnp.max(s.reshape(ch // V7X_SUBLANES, V7X_SUBLANES, s.shape[1]), axis=0)
        mx_ref[...] = cm if c == 0 else jnp.maximum(mx_ref[...], cm)

    @pl.when(pl.program_id(2) == 0)
    def _():
        for c in range(nch):
            scores_to_buffer(q_ref, c * ch, c)
        m_ref[...] = jnp.max(mx_ref[...], axis=0, keepdims=True)

    mprev_ref[...] = jnp.full_like(mprev_ref, -jnp.inf)
    acc_ref[...] = jnp.zeros_like(acc_ref)

    def step(t, qsrc_ref, next_row0, same_queries):
        for c in range(nch):
            p = jnp.exp2(s_ref[c * ch:(c + 1) * ch, :] - m_ref[...]).astype(BF16)
            scores_to_buffer(qsrc_ref, next_row0 + c * ch, c)
            key0 = t * step_keys + c * ch
            vt = vT_ref[0, 0, key0 // tk, :, pl.ds(pl.multiple_of(key0 % tk, ch), ch)]
            pv = _dot(jnp.concatenate([vt, ones], axis=0), p)
            if c == 0:
                acc_ref[...] = jnp.exp2(mprev_ref[...] - m_ref[...]) * acc_ref[...] + pv
            else:
                acc_ref[...] += pv
        new_max = jnp.max(mx_ref[...], axis=0, keepdims=True)
        if same_queries:
            m_t = m_ref[...]
            mprev_ref[...] = m_t
            m_ref[...] = jnp.maximum(m_t, new_max)
        else:
            m_ref[...] = new_max

    def body(t, carry):
        step(t, q_ref, (t + 1) * step_keys, True)
        return carry

    lax.fori_loop(0, nstep - 1, body, 0)
    step(nstep - 1, qn_ref, 0, False)
    o = acc_ref[0:hd, :] / acc_ref[hd:hd + 1, :]
    outs = [o[:, g * tq:(g + 1) * tq].T for g in range(G)]
    o_ref[0] = jnp.concatenate(outs, axis=1).astype(o_ref.dtype)


def _attention(qT, k, vT):
    B, _, L = qT.shape
    tq = Q_TILE
    nq = L // tq
    nk, tk = vT.shape[2], vT.shape[4]
    gw = ATT_GROUP * ATT_HD
    W = ATT_GROUP * tq
    step_keys = min(ATT_STEP, L)
    assert L % step_keys == 0 and step_keys % ATT_CHUNK == 0 and tk % ATT_CHUNK == 0
    return pl.pallas_call(
        functools.partial(_attn_kernel, step_keys=step_keys),
        grid=(B, ATT_KV_HEADS, nq),
        in_specs=[
            pl.BlockSpec((1, gw, tq), lambda b, h, i: (b, h, i)),
            pl.BlockSpec((1, gw, tq), lambda b, h, i: (b, h, jnp.minimum(i + 1, nq - 1))),
            pl.BlockSpec((1, 1, L, ATT_HD), lambda b, h, i: (b, h, 0, 0)),
            pl.BlockSpec((1, 1, nk, ATT_HD, tk), lambda b, h, i: (b, h, 0, 0, 0)),
        ],
        out_specs=pl.BlockSpec((1, tq, gw), lambda b, h, i: (b, i, h)),
        out_shape=jax.ShapeDtypeStruct((B, L, ATT_WIDTH), BF16),
        scratch_shapes=[pltpu.VMEM((ATT_HD, W), BF16), pltpu.VMEM((ATT_HD, W), BF16),
                        pltpu.VMEM((step_keys, W), F32),
                        pltpu.VMEM((V7X_SUBLANES, W), F32),
                        pltpu.VMEM((1, W), F32), pltpu.VMEM((1, W), F32),
                        pltpu.VMEM((ATT_HD + ATT_ONES_ROWS, W), F32)],
        compiler_params=pltpu.CompilerParams(
            dimension_semantics=("parallel", "parallel", "arbitrary"), vmem_limit_bytes=VMEM_LIMIT),
        name="attn",
    )(qT, qT, k, vT)


def _mix_ffn_kernel(ofp, ofc, ofn, obp, obc, obn, ggp, ggc, ggn, oap, oac, oan, xp, xc, xn,
                    on_ref, wo_ref, lnf_ref, wup_ref, cw_ref, cb_ref, wd_ref, lnw_ref, y_ref, act_ref):
    i = pl.program_id(1)
    tm = xc.shape[1]
    dff = wd_ref.shape[0]
    tf = FFN_COL_TILE
    rows = tm + 2 * HALO
    ext = lambda p, c, n: jnp.concatenate([p[0], c[0], n[0]], axis=0)

    att = _dot(ext(oap, oac, oan), wo_ref[GLA_WIDTH:, :])
    o = ext(ofp, ofc, ofn).astype(F32) + ext(obp, obc, obn).astype(F32)
    on = on_ref[...]
    normed = [_rms(o[:, h * GLA_DV:(h + 1) * GLA_DV], on) for h in range(GLA_HEADS)]
    half_g = 0.5 * ext(ggp, ggc, ggn)
    og = (jnp.concatenate(normed, axis=1) * (half_g * (1.0 + jnp.tanh(half_g)))).astype(BF16)
    x1 = ext(xp, xc, xn) + att + _dot(og, wo_ref[:GLA_WIDTH, :])
    h2 = _rms(x1, lnf_ref[...])
    r = lax.broadcasted_iota(jnp.int32, (rows, 1), 0)
    keep = jnp.logical_and(jnp.logical_or(i > 0, r >= HALO),
                           jnp.logical_or(i < pl.num_programs(1) - 1, r < HALO + tm))
    lhs = jnp.where(keep, h2, 0.0).astype(BF16)

    def up(j):
        return (_dot(lhs, wup_ref[:, j * tf:(j + 1) * tf]),
                _dot(lhs, wup_ref[:, dff + j * tf:dff + (j + 1) * tf]))

    def conv(u, col0):
        cw = cw_ref[:, col0:col0 + tf]
        prev = pltpu.roll(u, 1, 0)[HALO:HALO + tm]
        nxt = pltpu.roll(u, rows - 1, 0)[HALO:HALO + tm]
        return prev * cw[0:1] + u[HALO:HALO + tm] * cw[1:2] + nxt * cw[2:3] + cb_ref[:, col0:col0 + tf]

    n = dff // tf
    u_next = up(0)
    for j in range(n):
        ua, ug = u_next
        if j + 1 < n:
            u_next = up(j + 1)
        a = conv(ua, j * tf)
        half_gate = 0.5 * conv(ug, dff + j * tf)
        act_ref[:, j * tf:(j + 1) * tf] = (half_gate * (1.0 + jnp.tanh(half_gate)) * a).astype(BF16)
    y_ref[0] = _rms(x1[HALO:HALO + tm] + _dot(act_ref[...], wd_ref[...]), lnw_ref[...])


def _mix_ffn(o_f, o_b, gg, o_att, x, on, wo, lnf, w_up, conv_w, conv_b, w_down, lnw):
    B, L, D = x.shape
    tm = FFN_ROW_TILE
    hb = tm // HALO
    last_halo = L // HALO - 1
    full = lambda a: pl.BlockSpec(a.shape, lambda b, i: (0,) * a.ndim, pipeline_mode=pl.Buffered(1))

    def halo_tile_halo(w):
        return [pl.BlockSpec((1, HALO, w), lambda b, i: (b, jnp.maximum(i * hb - 1, 0), 0)),
                pl.BlockSpec((1, tm, w), lambda b, i: (b, i, 0)),
                pl.BlockSpec((1, HALO, w), lambda b, i: (b, jnp.minimum((i + 1) * hb, last_halo), 0))]

    acts = [o_f, o_b, gg, o_att, x]
    weights = [on, wo, lnf, w_up, conv_w, conv_b, w_down, lnw]
    return pl.pallas_call(
        _mix_ffn_kernel,
        grid=(B, L // tm),
        in_specs=sum([halo_tile_halo(a.shape[2]) for a in acts], []) + [full(w) for w in weights],
        out_specs=pl.BlockSpec((1, tm, D), lambda b, i: (b, i, 0)),
        out_shape=jax.ShapeDtypeStruct((B, L, D), F32),
        scratch_shapes=[pltpu.VMEM((tm, w_down.shape[0]), BF16)],
        compiler_params=pltpu.CompilerParams(
            dimension_semantics=("parallel", "parallel"), vmem_limit_bytes=VMEM_LIMIT),
        name="mix_ffn",
    )(*sum([[a, a, a] for a in acts], []), *weights)


def _rope_tables(L):
    rows = L // GRID_W
    row = jnp.repeat(jnp.arange(rows), GRID_W).astype(F32)
    col = jnp.tile(jnp.arange(GRID_W), rows).astype(F32)
    half = ATT_HD // 2
    inv = 1.0 / (ROPE_THETA ** (jnp.arange(0, half, 2, dtype=F32) / half))
    ang_r = inv[:, None] * row
    ang_c = inv[:, None] * col
    cr, sr, cc, sc = jnp.cos(ang_r), jnp.sin(ang_r), jnp.cos(ang_c), jnp.sin(ang_c)
    return jnp.concatenate([cr, cr, cc, cc], axis=0), jnp.concatenate([-sr, sr, -sc, sc], axis=0)


def _prepare(ln_mix, w_in, w_gk_fwd, b_gk_fwd, w_gk_bwd, b_gk_bwd, gla_out_norm, q_norm, k_norm,
             w_out, ln_ffn, w_up, conv_w, conv_b, w_down, ln_final, layer):
    pts = [0]
    for s in IN_SIZES:
        pts.append(pts[-1] + s)
    w = w_in[layer]
    piece = lambda a, b: w[:, pts[a]:pts[b]]
    zeros = jnp.zeros((GLA_RANK, GLA_QK_WIDTH), F32)
    sub = ROW_TILE // IN_PROJ_SUBTILES
    wgate = jnp.concatenate([jnp.concatenate([w_gk_fwd[layer], zeros], axis=1),
                             jnp.concatenate([zeros, w_gk_bwd[layer]], axis=1)], axis=0)
    return dict(
        lnw=ln_mix[layer][None, :],
        wg=piece(0, 4).astype(BF16),
        wa=piece(6, 9).astype(BF16),
        wr=piece(4, 6).astype(BF16),
        wgate=wgate.astype(BF16),
        bgate=jnp.concatenate([b_gk_fwd[layer], b_gk_bwd[layer]])[None, :],
        qn=jnp.broadcast_to(q_norm[layer][:, None], (ATT_HD, sub)),
        kn=jnp.broadcast_to(k_norm[layer][:, None], (ATT_HD, sub)),
        on=gla_out_norm[layer][None, :],
        wo=w_out[layer].astype(BF16),
        lnf=ln_ffn[layer][None, :],
        w_up=w_up[layer].astype(BF16),
        conv_w=conv_w[layer],
        conv_b=conv_b[layer][None, :],
        w_down=w_down[layer].astype(BF16),
    )


def _trunk(x, p, cos, sin, ln_final):
    gq, gk, gv, gg, gf, gb, qT, k, vT = _in_proj(
        x, p["lnw"], p["wg"], p["wa"], p["wr"], p["wgate"], p["bgate"], p["qn"], p["kn"], cos, sin)
    o_f, o_b = _gla(gq, gk, gv, gf, gb)
    o_att = _attention(qT, k, vT)
    return _mix_ffn(o_f, o_b, gg, o_att, x, p["on"], p["wo"], p["lnf"],
                    p["w_up"], p["conv_w"], p["conv_b"], p["w_down"], ln_final[None, :])


def kernel(x_prompt, x_sample, ln_mix, w_in, w_gk_fwd, b_gk_fwd, w_gk_bwd, b_gk_bwd, gla_out_norm,
           q_norm, k_norm, w_out, ln_ffn, w_up, conv_w, conv_b, w_down, ln_final):
    assert w_in.shape[0] == 1, "the mix_ffn kernel fuses the final RMSNorm: single-layer trunk only"
    p = _prepare(ln_mix, w_in, w_gk_fwd, b_gk_fwd, w_gk_bwd, b_gk_bwd, gla_out_norm, q_norm,
                 k_norm, w_out, ln_ffn, w_up, conv_w, conv_b, w_down, ln_final, 0)
    outs = []
    for x in (x_prompt, x_sample):
        cos, sin = _rope_tables(x.shape[1])
        outs.append(_trunk(x, p, cos, sin, ln_final))
    return tuple(outs)
```

```python
import functools

import jax
import jax.numpy as jnp
from jax import lax
from jax.experimental import pallas as pl
from jax.experimental.pallas import tpu as pltpu

F32 = jnp.float32
BF16 = jnp.bfloat16

GRID_W = 64
EPS = 1e-6
GLA_HEADS = 4
GLA_DK = 64
GLA_DV = 128
GLA_RANK = 16
GLA_GATE_NORM = 16.0
GLA_CHUNK = 64
GLA_QK_WIDTH = GLA_HEADS * GLA_DK
GLA_WIDTH = GLA_HEADS * GLA_DV
ATT_HEADS = 8
ATT_KV_HEADS = 2
ATT_GROUP = ATT_HEADS // ATT_KV_HEADS
ATT_HD = 64
ROPE_THETA = 10000.0
ATT_WIDTH = ATT_HEADS * ATT_HD
ATT_KV_WIDTH = ATT_KV_HEADS * ATT_HD
IN_SIZES = (GLA_QK_WIDTH, GLA_QK_WIDTH, GLA_WIDTH, GLA_WIDTH, GLA_RANK, GLA_RANK,
            ATT_WIDTH, ATT_KV_WIDTH, ATT_KV_WIDTH)

V7X_LANES = 128
V7X_SUBLANES = 8
V7X_VMEM_BYTES = 64 * 1024 * 1024
VMEM_LIMIT = V7X_VMEM_BYTES - 8 * 1024 * 1024

ROW_TILE = 512
IN_PROJ_SUBTILES = 2
Q_TILE = 128
ATT_STEP = 16384
ATT_CHUNK = 256
GLA_CHUNKS_PER_ITER = 4
GLA_CHAIN_STAGES = 4
FFN_ROW_TILE = 512
FFN_COL_TILE = 256
HALO = 16
LOG2E = 1.4426950408889634


def _dot(a, b):
    return jnp.dot(a, b, preferred_element_type=F32)


def _dot_nt(a, b):
    return lax.dot_general(a, b, (((1,), (1,)), ((), ())), preferred_element_type=F32)


def _dot_01(a01, x):
    x1 = x.astype(BF16)
    r1 = x - x1.astype(F32)
    x2 = r1.astype(BF16)
    x3 = (r1 - x2.astype(F32)).astype(BF16)
    return _dot(a01, x1) + _dot(a01, x2) + _dot(a01, x3)


def _rms(x, w):
    return x * lax.rsqrt(jnp.mean(x * x, axis=-1, keepdims=True) + EPS) * w


def _norm_rope_t(xt, gain, cos_t, sin_t):
    xn = xt * lax.rsqrt(jnp.mean(xt * xt, axis=0, keepdims=True) + EPS) * gain
    q4 = ATT_HD // 4
    partner = jnp.concatenate([xn[q4:2 * q4], xn[0:q4], xn[3 * q4:], xn[2 * q4:3 * q4]], axis=0)
    return xn * cos_t + partner * sin_t


def _in_proj_kernel(x_ref, lnw_ref, wg_ref, wa_ref, wr_ref, wgate_ref, bgate_ref,
                    qn_ref, kn_ref, cos_ref, sin_ref,
                    gq_ref, gk_ref, gv_ref, gg_ref, gf_ref, gb_ref, qT_ref, k_ref, vT_ref):
    tm = x_ref.shape[1]
    sub = tm // IN_PROJ_SUBTILES

    def projections(i):
        rows = slice(i * sub, (i + 1) * sub)
        hb = _rms(x_ref[0, rows, :], lnw_ref[...]).astype(BF16)
        return _dot(hb, wg_ref[...]), _dot(hb, wr_ref[...]), _dot(hb, wa_ref[...])

    def epilogue(i, pg, r, pa):
        rows = slice(i * sub, (i + 1) * sub)
        gq_ref[0, rows, :] = pg[:, 0:GLA_QK_WIDTH] * (GLA_DK ** -0.5)
        gk_ref[0, rows, :] = pg[:, GLA_QK_WIDTH:2 * GLA_QK_WIDTH]
        o = 2 * GLA_QK_WIDTH
        gv_ref[0, rows, :] = pg[:, o:o + GLA_WIDTH].astype(gv_ref.dtype)
        gg_ref[0, rows, :] = pg[:, o + GLA_WIDTH:o + 2 * GLA_WIDTH]

        z = _dot(r.astype(BF16), wgate_ref[...]) + bgate_ref[...]
        ls = (jnp.minimum(z, 0.0) - jnp.log1p(jnp.exp(-jnp.abs(z)))) * (1.0 / GLA_GATE_NORM)
        gf_ref[0, rows, :] = ls[:, :GLA_QK_WIDTH]
        gb_ref[0, rows, :] = ls[:, GLA_QK_WIDTH:]

        pat = pa.T
        cos_t = cos_ref[:, rows]
        sin_t = sin_ref[:, rows]
        for h in range(ATT_HEADS):
            qh = _norm_rope_t(pat[h * ATT_HD:(h + 1) * ATT_HD], qn_ref[...], cos_t, sin_t)
            qT_ref[0, h * ATT_HD:(h + 1) * ATT_HD, rows] = (qh * (LOG2E * ATT_HD ** -0.5)).astype(BF16)
        for h in range(ATT_KV_HEADS):
            k0 = ATT_WIDTH + h * ATT_HD
            kh = _norm_rope_t(pat[k0:k0 + ATT_HD], kn_ref[...], cos_t, sin_t)
            k_ref[0, h, rows, :] = kh.T.astype(BF16)
            v0 = ATT_WIDTH + ATT_KV_WIDTH + h * ATT_HD
            vT_ref[0, h, 0, :, rows] = pat[v0:v0 + ATT_HD].astype(BF16)

    nxt = projections(0)
    for i in range(IN_PROJ_SUBTILES):
        cur = nxt
        if i + 1 < IN_PROJ_SUBTILES:
            nxt = projections(i + 1)
        epilogue(i, *cur)


def _in_proj(x, lnw, wg, wa, wr, wgate, bgate, qn, kn, cos, sin):
    B, L, D = x.shape
    tm = ROW_TILE
    nt = L // tm
    full = lambda a: pl.BlockSpec(a.shape, lambda b, i: (0,) * a.ndim)
    row = lambda w: pl.BlockSpec((1, tm, w), lambda b, i: (b, i, 0))
    out_shape = (
        jax.ShapeDtypeStruct((B, L, GLA_QK_WIDTH), F32),
        jax.ShapeDtypeStruct((B, L, GLA_QK_WIDTH), F32),
        jax.ShapeDtypeStruct((B, L, GLA_WIDTH), BF16),
        jax.ShapeDtypeStruct((B, L, GLA_WIDTH), F32),
        jax.ShapeDtypeStruct((B, L, GLA_QK_WIDTH), F32),
        jax.ShapeDtypeStruct((B, L, GLA_QK_WIDTH), F32),
        jax.ShapeDtypeStruct((B, ATT_WIDTH, L), BF16),
        jax.ShapeDtypeStruct((B, ATT_KV_HEADS, L, ATT_HD), BF16),
        jax.ShapeDtypeStruct((B, ATT_KV_HEADS, nt, ATT_HD, tm), BF16),
    )
    out_specs = (
        row(GLA_QK_WIDTH), row(GLA_QK_WIDTH), row(GLA_WIDTH), row(GLA_WIDTH),
        row(GLA_QK_WIDTH), row(GLA_QK_WIDTH),
        pl.BlockSpec((1, ATT_WIDTH, tm), lambda b, i: (b, 0, i)),
        pl.BlockSpec((1, ATT_KV_HEADS, tm, ATT_HD), lambda b, i: (b, 0, i, 0)),
        pl.BlockSpec((1, ATT_KV_HEADS, 1, ATT_HD, tm), lambda b, i: (b, 0, i, 0, 0)),
    )
    tab = pl.BlockSpec((ATT_HD, tm), lambda b, i: (0, i))
    in_specs = [row(D), full(lnw), full(wg), full(wa), full(wr), full(wgate), full(bgate),
                full(qn), full(kn), tab, tab]
    return pl.pallas_call(
        _in_proj_kernel,
        grid=(B, nt),
        in_specs=in_specs,
        out_specs=out_specs,
        out_shape=out_shape,
        compiler_params=pltpu.CompilerParams(
            dimension_semantics=("parallel", "parallel"), vmem_limit_bytes=VMEM_LIMIT),
        name="in_proj",
    )(x, lnw, wg, wa, wr, wgate, bgate, qn, kn, cos, sin)


def _gla_chunk_stages(q_ref, k_ref, v_ref, g_ref, o_ref, c, tri, mask, last_row, head_of_lane, state):
    C = GLA_CHUNK
    heads = range(GLA_HEADS)
    sl = pl.ds(pl.multiple_of(c * C, C), C)
    q = q_ref[0, sl, :]
    k = k_ref[0, sl, :]
    v = v_ref[0, sl, :]
    vh = [v[:, h * GLA_DV:(h + 1) * GLA_DV] for h in heads]
    b = _dot_01(tri, g_ref[0, sl, :])
    yield
    btot = b[last_row:last_row + 1, :]
    qt = q * jnp.exp(b)
    kt = (k * jnp.exp(-b)).astype(BF16)
    kend_t = (k * jnp.exp(btot - b)).T.astype(BF16)
    decay = jnp.exp(jnp.broadcast_to(btot, (V7X_LANES, GLA_QK_WIDTH)).T)
    qms = [jnp.where(head_of_lane == h, qt, 0.0).astype(BF16) for h in heads]
    scores = [_dot_nt(qm, kt) for qm in qms]
    upd = [_dot(kend_t[h * GLA_DK:(h + 1) * GLA_DK, :], vh[h]) for h in heads]
    yield
    s_prev = state[0]
    s_prev_b = s_prev.astype(BF16)
    outs = [_dot(jnp.where(mask, scores[h], 0.0).astype(BF16), vh[h]) + _dot(qms[h], s_prev_b) for h in heads]
    state[0] = s_prev * decay + jnp.concatenate(upd, axis=0)
    yield
    o_ref[0, sl, :] = jnp.concatenate(outs, axis=1).astype(o_ref.dtype)
    yield


def _gla_kernel(qf, kf, vf, gf, qb, kb, vb, gb, of_ref, ob_ref, sf_ref, sb_ref):
    @pl.when(pl.program_id(1) == 0)
    def _():
        sf_ref[...] = jnp.zeros_like(sf_ref)
        sb_ref[...] = jnp.zeros_like(sb_ref)

    C = GLA_CHUNK
    nc = qf.shape[1] // C
    r = lax.broadcasted_iota(jnp.int32, (C, C), 0)
    cidx = lax.broadcasted_iota(jnp.int32, (C, C), 1)
    lower = r >= cidx
    upper = r <= cidx
    tri_f = lower.astype(BF16)
    tri_b = upper.astype(BF16)
    head_of_lane = lax.broadcasted_iota(jnp.int32, (C, GLA_QK_WIDTH), 1) // GLA_DK

    def body(it, carry):
        sf = [sf_ref[...]]
        sb = [sb_ref[...]]
        chains = []
        for u in range(GLA_CHUNKS_PER_ITER):
            c = it * GLA_CHUNKS_PER_ITER + u
            chains += [
                _gla_chunk_stages(qf, kf, vf, gf, of_ref, c, tri_f, lower, C - 1, head_of_lane, sf),
                _gla_chunk_stages(qb, kb, vb, gb, ob_ref, nc - 1 - c, tri_b, upper, 0, head_of_lane, sb)]
        for _ in range(GLA_CHAIN_STAGES):
            for chain in chains:
                next(chain)
        sf_ref[...] = sf[0]
        sb_ref[...] = sb[0]
        return carry

    lax.fori_loop(0, nc // GLA_CHUNKS_PER_ITER, body, 0)


def _gla(gq, gk, gv, gf, gb):
    B, L, _ = gq.shape
    T = ROW_TILE
    nt = L // T
    fwd = lambda w: pl.BlockSpec((1, T, w), lambda b, t: (b, t, 0))
    bwd = lambda w: pl.BlockSpec((1, T, w), lambda b, t: (b, nt - 1 - t, 0))
    qk, dv = GLA_QK_WIDTH, GLA_WIDTH
    return pl.pallas_call(
        _gla_kernel,
        grid=(B, nt),
        in_specs=[fwd(qk), fwd(qk), fwd(dv), fwd(qk), bwd(qk), bwd(qk), bwd(dv), bwd(qk)],
        out_specs=(fwd(dv), bwd(dv)),
        out_shape=(jax.ShapeDtypeStruct((B, L, dv), BF16), jax.ShapeDtypeStruct((B, L, dv), BF16)),
        scratch_shapes=[pltpu.VMEM((qk, GLA_DV), F32), pltpu.VMEM((qk, GLA_DV), F32)],
        compiler_params=pltpu.CompilerParams(
            dimension_semantics=("parallel", "arbitrary"), vmem_limit_bytes=VMEM_LIMIT),
        name="gla",
    )(gq, gk, gv, gf, gq, gk, gv, gb)


def _attn_kernel(qT_ref, qTn_ref, k_ref, vT_ref, o_ref, q_ref, qn_ref, s_ref, mx_ref, m_ref, mprev_ref, acc_ref,
                 *, step_keys):
    nkt, _, tk = vT_ref.shape[2:]
    G, hd = ATT_GROUP, ATT_HD
    tq = o_ref.shape[1]
    ch = ATT_CHUNK
    nstep = nkt * tk // step_keys
    nch = step_keys // ch
    for g in range(G):
        q_ref[:, g * tq:(g + 1) * tq] = qT_ref[0, g * hd:(g + 1) * hd, :]
        qn_ref[:, g * tq:(g + 1) * tq] = qTn_ref[0, g * hd:(g + 1) * hd, :]

    def scores_to_buffer(qsrc_ref, row0, c):
        s = _dot(k_ref[0, 0, pl.ds(pl.multiple_of(row0, ch), ch), :], qsrc_ref[...])
        s_ref[c * ch:(c + 1) * ch, :] = s
        cm = jnp.max(s.reshape(ch // V7X_SUBLANES, V7X_SUBLANES, s.shape[1]), axis=0)
        mx_ref[...] = cm if c == 0 else jnp.maximum(mx_ref[...], cm)

    @pl.when(pl.program_id(2) == 0)
    def _():
        for c in range(nch):
            scores_to_buffer(q_ref, c * ch, c)
        m_ref[...] = jnp.max(mx_ref[...], axis=0, keepdims=True)

    mprev_ref[...] = jnp.full_like(mprev_ref, -jnp.inf)
    acc_ref[...] = jnp.zeros_like(acc_ref)

    def step(t, qsrc_ref, next_row0, same_queries):
        for c in range(nch):
            pf = jnp.exp2(s_ref[c * ch:(c + 1) * ch, :] - m_ref[...])
            lsum = jnp.sum(pf.reshape(ch // V7X_SUBLANES, V7X_SUBLANES, pf.shape[1]), axis=0)
            p = pf.astype(BF16)
            scores_to_buffer(qsrc_ref, next_row0 + c * ch, c)
            key0 = t * step_keys + c * ch
            vt = vT_ref[0, 0, key0 // tk, :, pl.ds(pl.multiple_of(key0 % tk, ch), ch)]
            pv = _dot(vt, p)
            if c == 0:
                alpha = jnp.exp2(mprev_ref[...] - m_ref[...])
                acc_ref[0:hd, :] = alpha * acc_ref[0:hd, :] + pv
                acc_ref[hd:hd + V7X_SUBLANES, :] = alpha * acc_ref[hd:hd + V7X_SUBLANES, :] + lsum
            else:
                acc_ref[0:hd, :] += pv
                acc_ref[hd:hd + V7X_SUBLANES, :] += lsum
        new_max = jnp.max(mx_ref[...], axis=0, keepdims=True)
        if same_queries:
            m_t = m_ref[...]
            mprev_ref[...] = m_t
            m_ref[...] = jnp.maximum(m_t, new_max)
        else:
            m_ref[...] = new_max

    def body(t, carry):
        step(t, q_ref, (t + 1) * step_keys, True)
        return carry

    lax.fori_loop(0, nstep - 1, body, 0)
    step(nstep - 1, qn_ref, 0, False)
    o = acc_ref[0:hd, :] / jnp.sum(acc_ref[hd:hd + V7X_SUBLANES, :], axis=0, keepdims=True)
    outs = [o[:, g * tq:(g + 1) * tq].T for g in range(G)]
    o_ref[0] = jnp.concatenate(outs, axis=1).astype(o_ref.dtype)


def _attention(qT, k, vT):
    B, _, L = qT.shape
    tq = Q_TILE
    nq = L // tq
    nk, tk = vT.shape[2], vT.shape[4]
    gw = ATT_GROUP * ATT_HD
    W = ATT_GROUP * tq
    step_keys = min(ATT_STEP, L)
    assert L % step_keys == 0 and step_keys % ATT_CHUNK == 0 and tk % ATT_CHUNK == 0
    return pl.pallas_call(
        functools.partial(_attn_kernel, step_keys=step_keys),
        grid=(B, ATT_KV_HEADS, nq),
        in_specs=[
            pl.BlockSpec((1, gw, tq), lambda b, h, i: (b, h, i)),
            pl.BlockSpec((1, gw, tq), lambda b, h, i: (b, h, jnp.minimum(i + 1, nq - 1))),
            pl.BlockSpec((1, 1, L, ATT_HD), lambda b, h, i: (b, h, 0, 0)),
            pl.BlockSpec((1, 1, nk, ATT_HD, tk), lambda b, h, i: (b, h, 0, 0, 0)),
        ],
        out_specs=pl.BlockSpec((1, tq, gw), lambda b, h, i: (b, i, h)),
        out_shape=jax.ShapeDtypeStruct((B, L, ATT_WIDTH), BF16),
        scratch_shapes=[pltpu.VMEM((ATT_HD, W), BF16), pltpu.VMEM((ATT_HD, W), BF16),
                        pltpu.VMEM((step_keys, W), F32),
                        pltpu.VMEM((V7X_SUBLANES, W), F32),
                        pltpu.VMEM((1, W), F32), pltpu.VMEM((1, W), F32),
                        pltpu.VMEM((ATT_HD + V7X_SUBLANES, W), F32)],
        compiler_params=pltpu.CompilerParams(
            dimension_semantics=("parallel", "parallel", "arbitrary"), vmem_limit_bytes=VMEM_LIMIT),
        name="attn",
    )(qT, qT, k, vT)


def _mix_ffn_kernel(ofp, ofc, ofn, obp, obc, obn, ggp, ggc, ggn, oap, oac, oan, xp, xc, xn,
                    on_ref, wo_ref, lnf_ref, wup_ref, cw_ref, cb_ref, wd_ref, lnw_ref, y_ref, act_ref):
    i = pl.program_id(1)
    tm = xc.shape[1]
    dff = wd_ref.shape[0]
    tf = FFN_COL_TILE
    rows = tm + 2 * HALO
    ext = lambda p, c, n: jnp.concatenate([p[0], c[0], n[0]], axis=0)

    att = _dot(ext(oap, oac, oan), wo_ref[GLA_WIDTH:, :])
    o = ext(ofp, ofc, ofn).astype(F32) + ext(obp, obc, obn).astype(F32)
    on = on_ref[...]
    normed = [_rms(o[:, h * GLA_DV:(h + 1) * GLA_DV], on) for h in range(GLA_HEADS)]
    half_g = 0.5 * ext(ggp, ggc, ggn)
    og = (jnp.concatenate(normed, axis=1) * (half_g * (1.0 + jnp.tanh(half_g)))).astype(BF16)
    x1 = ext(xp, xc, xn) + att + _dot(og, wo_ref[:GLA_WIDTH, :])
    h2 = _rms(x1, lnf_ref[...])
    r = lax.broadcasted_iota(jnp.int32, (rows, 1), 0)
    keep = jnp.logical_and(jnp.logical_or(i > 0, r >= HALO),
                           jnp.logical_or(i < pl.num_programs(1) - 1, r < HALO + tm))
    lhs = jnp.where(keep, h2, 0.0).astype(BF16)

    def up(j):
        return (_dot(lhs, wup_ref[:, j * tf:(j + 1) * tf]),
                _dot(lhs, wup_ref[:, dff + j * tf:dff + (j + 1) * tf]))

    def conv(u, col0):
        cw = cw_ref[:, col0:col0 + tf]
        prev = pltpu.roll(u, 1, 0)[HALO:HALO + tm]
        nxt = pltpu.roll(u, rows - 1, 0)[HALO:HALO + tm]
        return prev * cw[0:1] + u[HALO:HALO + tm] * cw[1:2] + nxt * cw[2:3] + cb_ref[:, col0:col0 + tf]

    n = dff // tf
    u_next = up(0)
    for j in range(n):
        ua, ug = u_next
        if j + 1 < n:
            u_next = up(j + 1)
        a = conv(ua, j * tf)
        half_gate = 0.5 * conv(ug, dff + j * tf)
        act_ref[:, j * tf:(j + 1) * tf] = (half_gate * (1.0 + jnp.tanh(half_gate)) * a).astype(BF16)
    y_ref[0] = _rms(x1[HALO:HALO + tm] + _dot(act_ref[...], wd_ref[...]), lnw_ref[...])


def _mix_ffn(o_f, o_b, gg, o_att, x, on, wo, lnf, w_up, conv_w, conv_b, w_down, lnw):
    B, L, D = x.shape
    tm = FFN_ROW_TILE
    hb = tm // HALO
    last_halo = L // HALO - 1
    full = lambda a: pl.BlockSpec(a.shape, lambda b, i: (0,) * a.ndim, pipeline_mode=pl.Buffered(1))

    def halo_tile_halo(w):
        return [pl.BlockSpec((1, HALO, w), lambda b, i: (b, jnp.maximum(i * hb - 1, 0), 0)),
                pl.BlockSpec((1, tm, w), lambda b, i: (b, i, 0)),
                pl.BlockSpec((1, HALO, w), lambda b, i: (b, jnp.minimum((i + 1) * hb, last_halo), 0))]

    acts = [o_f, o_b, gg, o_att, x]
    weights = [on, wo, lnf, w_up, conv_w, conv_b, w_down, lnw]
    return pl.pallas_call(
        _mix_ffn_kernel,
        grid=(B, L // tm),
        in_specs=sum([halo_tile_halo(a.shape[2]) for a in acts], []) + [full(w) for w in weights],
        out_specs=pl.BlockSpec((1, tm, D), lambda b, i: (b, i, 0)),
        out_shape=jax.ShapeDtypeStruct((B, L, D), F32),
        scratch_shapes=[pltpu.VMEM((tm, w_down.shape[0]), BF16)],
        compiler_params=pltpu.CompilerParams(
            dimension_semantics=("parallel", "parallel"), vmem_limit_bytes=VMEM_LIMIT),
        name="mix_ffn",
    )(*sum([[a, a, a] for a in acts], []), *weights)


def _rope_tables(L):
    rows = L // GRID_W
    row = jnp.repeat(jnp.arange(rows), GRID_W).astype(F32)
    col = jnp.tile(jnp.arange(GRID_W), rows).astype(F32)
    half = ATT_HD // 2
    inv = 1.0 / (ROPE_THETA ** (jnp.arange(0, half, 2, dtype=F32) / half))
    ang_r = inv[:, None] * row
    ang_c = inv[:, None] * col
    cr, sr, cc, sc = jnp.cos(ang_r), jnp.sin(ang_r), jnp.cos(ang_c), jnp.sin(ang_c)
    return jnp.concatenate([cr, cr, cc, cc], axis=0), jnp.concatenate([-sr, sr, -sc, sc], axis=0)


def _prepare(ln_mix, w_in, w_gk_fwd, b_gk_fwd, w_gk_bwd, b_gk_bwd, gla_out_norm, q_norm, k_norm,
             w_out, ln_ffn, w_up, conv_w, conv_b, w_down, ln_final, layer):
    pts = [0]
    for s in IN_SIZES:
        pts.append(pts[-1] + s)
    w = w_in[layer]
    piece = lambda a, b: w[:, pts[a]:pts[b]]
    zeros = jnp.zeros((GLA_RANK, GLA_QK_WIDTH), F32)
    sub = ROW_TILE // IN_PROJ_SUBTILES
    wgate = jnp.concatenate([jnp.concatenate([w_gk_fwd[layer], zeros], axis=1),
                             jnp.concatenate([zeros, w_gk_bwd[layer]], axis=1)], axis=0)
    return dict(
        lnw=ln_mix[layer][None, :],
        wg=piece(0, 4).astype(BF16),
        wa=piece(6, 9).astype(BF16),
        wr=piece(4, 6).astype(BF16),
        wgate=wgate.astype(BF16),
        bgate=jnp.concatenate([b_gk_fwd[layer], b_gk_bwd[layer]])[None, :],
        qn=jnp.broadcast_to(q_norm[layer][:, None], (ATT_HD, sub)),
        kn=jnp.broadcast_to(k_norm[layer][:, None], (ATT_HD, sub)),
        on=gla_out_norm[layer][None, :],
        wo=w_out[layer].astype(BF16),
        lnf=ln_ffn[layer][None, :],
        w_up=w_up[layer].astype(BF16),
        conv_w=conv_w[layer],
        conv_b=conv_b[layer][None, :],
        w_down=w_down[layer].astype(BF16),
    )


def _trunk(x, p, cos, sin, ln_final):
    gq, gk, gv, gg, gf, gb, qT, k, vT = _in_proj(
        x, p["lnw"], p["wg"], p["wa"], p["wr"], p["wgate"], p["bgate"], p["qn"], p["kn"], cos, sin)
    o_f, o_b = _gla(gq, gk, gv, gf, gb)
    o_att = _attention(qT, k, vT)
    return _mix_ffn(o_f, o_b, gg, o_att, x, p["on"], p["wo"], p["lnf"],
                    p["w_up"], p["conv_w"], p["conv_b"], p["w_down"], ln_final[None, :])


def kernel(x_prompt, x_sample, ln_mix, w_in, w_gk_fwd, b_gk_fwd, w_gk_bwd, b_gk_bwd, gla_out_norm,
           q_norm, k_norm, w_out, ln_ffn, w_up, conv_w, conv_b, w_down, ln_final):
    assert w_in.shape[0] == 1, "the mix_ffn kernel fuses the final RMSNorm: single-layer trunk only"
    p = _prepare(ln_mix, w_in, w_gk_fwd, b_gk_fwd, w_gk_bwd, b_gk_bwd, gla_out_norm, q_norm,
                 k_norm, w_out, ln_ffn, w_up, conv_w, conv_b, w_down, ln_final, 0)
    outs = []
    for x in (x_prompt, x_sample):
        cos, sin = _rope_tables(x.shape[1])
        outs.append(_trunk(x, p, cos, sin, ln_final))
    return tuple(outs)
```

```python
import functools

import jax
import jax.numpy as jnp
from jax import lax
from jax.experimental import pallas as pl
from jax.experimental.pallas import tpu as pltpu

F32 = jnp.float32
BF16 = jnp.bfloat16

GRID_W = 64
EPS = 1e-6
GLA_HEADS = 4
GLA_DK = 64
GLA_DV = 128
GLA_RANK = 16
GLA_GATE_NORM = 16.0
GLA_CHUNK = 64
GLA_QK_WIDTH = GLA_HEADS * GLA_DK
GLA_WIDTH = GLA_HEADS * GLA_DV
ATT_HEADS = 8
ATT_KV_HEADS = 2
ATT_GROUP = ATT_HEADS // ATT_KV_HEADS
ATT_HD = 64
ROPE_THETA = 10000.0
ATT_WIDTH = ATT_HEADS * ATT_HD
ATT_KV_WIDTH = ATT_KV_HEADS * ATT_HD
IN_SIZES = (GLA_QK_WIDTH, GLA_QK_WIDTH, GLA_WIDTH, GLA_WIDTH, GLA_RANK, GLA_RANK,
            ATT_WIDTH, ATT_KV_WIDTH, ATT_KV_WIDTH)

V7X_LANES = 128
V7X_SUBLANES = 8
V7X_VMEM_BYTES = 64 * 1024 * 1024
VMEM_LIMIT = V7X_VMEM_BYTES - 8 * 1024 * 1024

ROW_TILE = 512
IN_PROJ_SUBTILES = 2
Q_TILE = 128
ATT_STEP = 16384
ATT_CHUNK = 256
GLA_CHUNKS_PER_ITER = 4
GLA_CHAIN_STAGES = 4
FFN_ROW_TILE = 512
FFN_COL_TILE = 256
FFN_ROW_SPLIT = 4
HALO = 16
LOG2E = 1.4426950408889634


def _dot(a, b):
    return jnp.dot(a, b, preferred_element_type=F32)


def _dot_nt(a, b):
    return lax.dot_general(a, b, (((1,), (1,)), ((), ())), preferred_element_type=F32)


def _dot_01(a01, x):
    x1 = x.astype(BF16)
    r1 = x - x1.astype(F32)
    x2 = r1.astype(BF16)
    x3 = (r1 - x2.astype(F32)).astype(BF16)
    return _dot(a01, x1) + _dot(a01, x2) + _dot(a01, x3)


def _rms(x, w):
    return x * lax.rsqrt(jnp.mean(x * x, axis=-1, keepdims=True) + EPS) * w


def _norm_rope_t(xt, gain, cos_t, sin_t):
    xn = xt * lax.rsqrt(jnp.mean(xt * xt, axis=0, keepdims=True) + EPS) * gain
    q4 = ATT_HD // 4
    partner = jnp.concatenate([xn[q4:2 * q4], xn[0:q4], xn[3 * q4:], xn[2 * q4:3 * q4]], axis=0)
    return xn * cos_t + partner * sin_t


def _in_proj_kernel(x_ref, lnw_ref, wg_ref, wa_ref, wr_ref, wgate_ref, bgate_ref,
                    qn_ref, kn_ref, cos_ref, sin_ref,
                    gq_ref, gk_ref, gv_ref, gg_ref, gf_ref, gb_ref, qT_ref, k_ref, vT_ref):
    tm = x_ref.shape[1]
    sub = tm // IN_PROJ_SUBTILES

    def projections(i):
        rows = slice(i * sub, (i + 1) * sub)
        hb = _rms(x_ref[0, rows, :], lnw_ref[...]).astype(BF16)
        return _dot(hb, wg_ref[...]), _dot(hb, wr_ref[...]), _dot(hb, wa_ref[...])

    def epilogue(i, pg, r, pa):
        rows = slice(i * sub, (i + 1) * sub)
        gq_ref[0, rows, :] = pg[:, 0:GLA_QK_WIDTH] * (GLA_DK ** -0.5)
        gk_ref[0, rows, :] = pg[:, GLA_QK_WIDTH:2 * GLA_QK_WIDTH]
        o = 2 * GLA_QK_WIDTH
        gv_ref[0, rows, :] = pg[:, o:o + GLA_WIDTH].astype(gv_ref.dtype)
        gg_ref[0, rows, :] = pg[:, o + GLA_WIDTH:o + 2 * GLA_WIDTH]

        z = _dot(r.astype(BF16), wgate_ref[...]) + bgate_ref[...]
        ls = (jnp.minimum(z, 0.0) - jnp.log1p(jnp.exp(-jnp.abs(z)))) * (1.0 / GLA_GATE_NORM)
        gf_ref[0, rows, :] = ls[:, :GLA_QK_WIDTH]
        gb_ref[0, rows, :] = ls[:, GLA_QK_WIDTH:]

        pat = pa.T
        cos_t = cos_ref[:, rows]
        sin_t = sin_ref[:, rows]
        for h in range(ATT_HEADS):
            qh = _norm_rope_t(pat[h * ATT_HD:(h + 1) * ATT_HD], qn_ref[...], cos_t, sin_t)
            qT_ref[0, h * ATT_HD:(h + 1) * ATT_HD, rows] = (qh * (LOG2E * ATT_HD ** -0.5)).astype(BF16)
        for h in range(ATT_KV_HEADS):
            k0 = ATT_WIDTH + h * ATT_HD
            kh = _norm_rope_t(pat[k0:k0 + ATT_HD], kn_ref[...], cos_t, sin_t)
            k_ref[0, h, rows, :] = kh.T.astype(BF16)
            v0 = ATT_WIDTH + ATT_KV_WIDTH + h * ATT_HD
            vT_ref[0, h, 0, :, rows] = pat[v0:v0 + ATT_HD].astype(BF16)

    nxt = projections(0)
    for i in range(IN_PROJ_SUBTILES):
        cur = nxt
        if i + 1 < IN_PROJ_SUBTILES:
            nxt = projections(i + 1)
        epilogue(i, *cur)


def _in_proj(x, lnw, wg, wa, wr, wgate, bgate, qn, kn, cos, sin):
    B, L, D = x.shape
    tm = ROW_TILE
    nt = L // tm
    full = lambda a: pl.BlockSpec(a.shape, lambda b, i: (0,) * a.ndim)
    row = lambda w: pl.BlockSpec((1, tm, w), lambda b, i: (b, i, 0))
    out_shape = (
        jax.ShapeDtypeStruct((B, L, GLA_QK_WIDTH), F32),
        jax.ShapeDtypeStruct((B, L, GLA_QK_WIDTH), F32),
        jax.ShapeDtypeStruct((B, L, GLA_WIDTH), BF16),
        jax.ShapeDtypeStruct((B, L, GLA_WIDTH), F32),
        jax.ShapeDtypeStruct((B, L, GLA_QK_WIDTH), F32),
        jax.ShapeDtypeStruct((B, L, GLA_QK_WIDTH), F32),
        jax.ShapeDtypeStruct((B, ATT_WIDTH, L), BF16),
        jax.ShapeDtypeStruct((B, ATT_KV_HEADS, L, ATT_HD), BF16),
        jax.ShapeDtypeStruct((B, ATT_KV_HEADS, nt, ATT_HD, tm), BF16),
    )
    out_specs = (
        row(GLA_QK_WIDTH), row(GLA_QK_WIDTH), row(GLA_WIDTH), row(GLA_WIDTH),
        row(GLA_QK_WIDTH), row(GLA_QK_WIDTH),
        pl.BlockSpec((1, ATT_WIDTH, tm), lambda b, i: (b, 0, i)),
        pl.BlockSpec((1, ATT_KV_HEADS, tm, ATT_HD), lambda b, i: (b, 0, i, 0)),
        pl.BlockSpec((1, ATT_KV_HEADS, 1, ATT_HD, tm), lambda b, i: (b, 0, i, 0, 0)),
    )
    tab = pl.BlockSpec((ATT_HD, tm), lambda b, i: (0, i))
    in_specs = [row(D), full(lnw), full(wg), full(wa), full(wr), full(wgate), full(bgate),
                full(qn), full(kn), tab, tab]
    return pl.pallas_call(
        _in_proj_kernel,
        grid=(B, nt),
        in_specs=in_specs,
        out_specs=out_specs,
        out_shape=out_shape,
        compiler_params=pltpu.CompilerParams(
            dimension_semantics=("parallel", "parallel"), vmem_limit_bytes=VMEM_LIMIT),
        name="in_proj",
    )(x, lnw, wg, wa, wr, wgate, bgate, qn, kn, cos, sin)


def _gla_chunk_stages(q_ref, k_ref, v_ref, g_ref, o_ref, c, tri, mask, last_row, head_of_lane, state):
    C = GLA_CHUNK
    heads = range(GLA_HEADS)
    sl = pl.ds(pl.multiple_of(c * C, C), C)
    q = q_ref[0, sl, :]
    k = k_ref[0, sl, :]
    v = v_ref[0, sl, :]
    vh = [v[:, h * GLA_DV:(h + 1) * GLA_DV] for h in heads]
    b = _dot_01(tri, g_ref[0, sl, :])
    yield
    btot = b[last_row:last_row + 1, :]
    qt = q * jnp.exp(b)
    kt = (k * jnp.exp(-b)).astype(BF16)
    kend_t = (k * jnp.exp(btot - b)).T.astype(BF16)
    decay = jnp.exp(jnp.broadcast_to(btot, (V7X_LANES, GLA_QK_WIDTH)).T)
    qms = [jnp.where(head_of_lane == h, qt, 0.0).astype(BF16) for h in heads]
    scores = [_dot_nt(qm, kt) for qm in qms]
    upd = [_dot(kend_t[h * GLA_DK:(h + 1) * GLA_DK, :], vh[h]) for h in heads]
    yield
    s_prev = state[0]
    s_prev_b = s_prev.astype(BF16)
    outs = [_dot(jnp.where(mask, scores[h], 0.0).astype(BF16), vh[h]) + _dot(qms[h], s_prev_b) for h in heads]
    state[0] = s_prev * decay + jnp.concatenate(upd, axis=0)
    yield
    o_ref[0, sl, :] = jnp.concatenate(outs, axis=1).astype(o_ref.dtype)
    yield


def _gla_kernel(qf, kf, vf, gf, qb, kb, vb, gb, of_ref, ob_ref, sf_ref, sb_ref):
    @pl.when(pl.program_id(1) == 0)
    def _():
        sf_ref[...] = jnp.zeros_like(sf_ref)
        sb_ref[...] = jnp.zeros_like(sb_ref)

    C = GLA_CHUNK
    nc = qf.shape[1] // C
    r = lax.broadcasted_iota(jnp.int32, (C, C), 0)
    cidx = lax.broadcasted_iota(jnp.int32, (C, C), 1)
    lower = r >= cidx
    upper = r <= cidx
    tri_f = lower.astype(BF16)
    tri_b = upper.astype(BF16)
    head_of_lane = lax.broadcasted_iota(jnp.int32, (C, GLA_QK_WIDTH), 1) // GLA_DK

    def body(it, carry):
        sf = [sf_ref[...]]
        sb = [sb_ref[...]]
        chains = []
        for u in range(GLA_CHUNKS_PER_ITER):
            c = it * GLA_CHUNKS_PER_ITER + u
            chains += [
                _gla_chunk_stages(qf, kf, vf, gf, of_ref, c, tri_f, lower, C - 1, head_of_lane, sf),
                _gla_chunk_stages(qb, kb, vb, gb, ob_ref, nc - 1 - c, tri_b, upper, 0, head_of_lane, sb)]
        for _ in range(GLA_CHAIN_STAGES):
            for chain in chains:
                next(chain)
        sf_ref[...] = sf[0]
        sb_ref[...] = sb[0]
        return carry

    lax.fori_loop(0, nc // GLA_CHUNKS_PER_ITER, body, 0)


def _gla(gq, gk, gv, gf, gb):
    B, L, _ = gq.shape
    T = ROW_TILE
    nt = L // T
    fwd = lambda w: pl.BlockSpec((1, T, w), lambda b, t: (b, t, 0))
    bwd = lambda w: pl.BlockSpec((1, T, w), lambda b, t: (b, nt - 1 - t, 0))
    qk, dv = GLA_QK_WIDTH, GLA_WIDTH
    return pl.pallas_call(
        _gla_kernel,
        grid=(B, nt),
        in_specs=[fwd(qk), fwd(qk), fwd(dv), fwd(qk), bwd(qk), bwd(qk), bwd(dv), bwd(qk)],
        out_specs=(fwd(dv), bwd(dv)),
        out_shape=(jax.ShapeDtypeStruct((B, L, dv), BF16), jax.ShapeDtypeStruct((B, L, dv), BF16)),
        scratch_shapes=[pltpu.VMEM((qk, GLA_DV), F32), pltpu.VMEM((qk, GLA_DV), F32)],
        compiler_params=pltpu.CompilerParams(
            dimension_semantics=("parallel", "arbitrary"), vmem_limit_bytes=VMEM_LIMIT),
        name="gla",
    )(gq, gk, gv, gf, gq, gk, gv, gb)


def _attn_kernel(qT_ref, qTn_ref, k_ref, vT_ref, o_ref, q_ref, qn_ref, s_ref, mx_ref, m_ref, mprev_ref, acc_ref,
                 *, step_keys):
    nkt, _, tk = vT_ref.shape[2:]
    G, hd = ATT_GROUP, ATT_HD
    tq = o_ref.shape[1]
    ch = ATT_CHUNK
    nstep = nkt * tk // step_keys
    nch = step_keys // ch
    for g in range(G):
        q_ref[:, g * tq:(g + 1) * tq] = qT_ref[0, g * hd:(g + 1) * hd, :]
        qn_ref[:, g * tq:(g + 1) * tq] = qTn_ref[0, g * hd:(g + 1) * hd, :]

    def scores_to_buffer(qsrc_ref, row0, c):
        s = _dot(k_ref[0, 0, pl.ds(pl.multiple_of(row0, ch), ch), :], qsrc_ref[...])
        s_ref[c * ch:(c + 1) * ch, :] = s
        cm = jnp.max(s.reshape(ch // V7X_SUBLANES, V7X_SUBLANES, s.shape[1]), axis=0)
        mx_ref[...] = cm if c == 0 else jnp.maximum(mx_ref[...], cm)

    @pl.when(pl.program_id(2) == 0)
    def _():
        for c in range(nch):
            scores_to_buffer(q_ref, c * ch, c)
        m_ref[...] = jnp.max(mx_ref[...], axis=0, keepdims=True)

    mprev_ref[...] = jnp.full_like(mprev_ref, -jnp.inf)
    acc_ref[...] = jnp.zeros_like(acc_ref)

    def step(t, qsrc_ref, next_row0, same_queries):
        for c in range(nch):
            pf = jnp.exp2(s_ref[c * ch:(c + 1) * ch, :] - m_ref[...])
            lsum = jnp.sum(pf.reshape(ch // V7X_SUBLANES, V7X_SUBLANES, pf.shape[1]), axis=0)
            p = pf.astype(BF16)
            scores_to_buffer(qsrc_ref, next_row0 + c * ch, c)
            key0 = t * step_keys + c * ch
            vt = vT_ref[0, 0, key0 // tk, :, pl.ds(pl.multiple_of(key0 % tk, ch), ch)]
            pv = _dot(vt, p)
            if c == 0:
                alpha = jnp.exp2(mprev_ref[...] - m_ref[...])
                acc_ref[0:hd, :] = alpha * acc_ref[0:hd, :] + pv
                acc_ref[hd:hd + V7X_SUBLANES, :] = alpha * acc_ref[hd:hd + V7X_SUBLANES, :] + lsum
            else:
                acc_ref[0:hd, :] += pv
                acc_ref[hd:hd + V7X_SUBLANES, :] += lsum
        new_max = jnp.max(mx_ref[...], axis=0, keepdims=True)
        if same_queries:
            m_t = m_ref[...]
            mprev_ref[...] = m_t
            m_ref[...] = jnp.maximum(m_t, new_max)
        else:
            m_ref[...] = new_max

    def body(t, carry):
        step(t, q_ref, (t + 1) * step_keys, True)
        return carry

    lax.fori_loop(0, nstep - 1, body, 0)
    step(nstep - 1, qn_ref, 0, False)
    o = acc_ref[0:hd, :] / jnp.sum(acc_ref[hd:hd + V7X_SUBLANES, :], axis=0, keepdims=True)
    outs = [o[:, g * tq:(g + 1) * tq].T for g in range(G)]
    o_ref[0] = jnp.concatenate(outs, axis=1).astype(o_ref.dtype)


def _attention(qT, k, vT):
    B, _, L = qT.shape
    tq = Q_TILE
    nq = L // tq
    nk, tk = vT.shape[2], vT.shape[4]
    gw = ATT_GROUP * ATT_HD
    W = ATT_GROUP * tq
    step_keys = min(ATT_STEP, L)
    assert L % step_keys == 0 and step_keys % ATT_CHUNK == 0 and tk % ATT_CHUNK == 0
    return pl.pallas_call(
        functools.partial(_attn_kernel, step_keys=step_keys),
        grid=(B, ATT_KV_HEADS, nq),
        in_specs=[
            pl.BlockSpec((1, gw, tq), lambda b, h, i: (b, h, i)),
            pl.BlockSpec((1, gw, tq), lambda b, h, i: (b, h, jnp.minimum(i + 1, nq - 1))),
            pl.BlockSpec((1, 1, L, ATT_HD), lambda b, h, i: (b, h, 0, 0)),
            pl.BlockSpec((1, 1, nk, ATT_HD, tk), lambda b, h, i: (b, h, 0, 0, 0)),
        ],
        out_specs=pl.BlockSpec((1, tq, gw), lambda b, h, i: (b, i, h)),
        out_shape=jax.ShapeDtypeStruct((B, L, ATT_WIDTH), BF16),
        scratch_shapes=[pltpu.VMEM((ATT_HD, W), BF16), pltpu.VMEM((ATT_HD, W), BF16),
                        pltpu.VMEM((step_keys, W), F32),
                        pltpu.VMEM((V7X_SUBLANES, W), F32),
                        pltpu.VMEM((1, W), F32), pltpu.VMEM((1, W), F32),
                        pltpu.VMEM((ATT_HD + V7X_SUBLANES, W), F32)],
        compiler_params=pltpu.CompilerParams(
            dimension_semantics=("parallel", "parallel", "arbitrary"), vmem_limit_bytes=VMEM_LIMIT),
        name="attn",
    )(qT, qT, k, vT)


def _mix_ffn_kernel(ofp, ofc, ofn, obp, obc, obn, ggp, ggc, ggn, oap, oac, oan, xp, xc, xn,
                    on_ref, wo_ref, lnf_ref, wup_ref, cw_ref, cb_ref, wd_ref, lnw_ref, y_ref, act_ref):
    i = pl.program_id(1)
    tm = xc.shape[1]
    dff = wd_ref.shape[0]
    tf = FFN_COL_TILE
    rows = tm + 2 * HALO
    ext = lambda p, c, n: jnp.concatenate([p[0], c[0], n[0]], axis=0)

    att = _dot(ext(oap, oac, oan), wo_ref[GLA_WIDTH:, :])
    o = ext(ofp, ofc, ofn).astype(F32) + ext(obp, obc, obn).astype(F32)
    on = on_ref[...]
    normed = [_rms(o[:, h * GLA_DV:(h + 1) * GLA_DV], on) for h in range(GLA_HEADS)]
    half_g = 0.5 * ext(ggp, ggc, ggn)
    og = (jnp.concatenate(normed, axis=1) * (half_g * (1.0 + jnp.tanh(half_g)))).astype(BF16)
    x1 = ext(xp, xc, xn) + att + _dot(og, wo_ref[:GLA_WIDTH, :])
    h2 = _rms(x1, lnf_ref[...])
    r = lax.broadcasted_iota(jnp.int32, (rows, 1), 0)
    keep = jnp.logical_and(jnp.logical_or(i > 0, r >= HALO),
                           jnp.logical_or(i < pl.num_programs(1) - 1, r < HALO + tm))
    lhs = jnp.where(keep, h2, 0.0).astype(BF16)

    def up(j):
        return (_dot(lhs, wup_ref[:, j * tf:(j + 1) * tf]),
                _dot(lhs, wup_ref[:, dff + j * tf:dff + (j + 1) * tf]))

    def conv(u, u_prev, u_next_row, col0, r0, nr):
        cw = cw_ref[:, col0:col0 + tf]
        sl = slice(HALO + r0, HALO + r0 + nr)
        return u_prev[sl] * cw[0:1] + u[sl] * cw[1:2] + u_next_row[sl] * cw[2:3] + cb_ref[:, col0:col0 + tf]

    n = dff // tf
    nr = tm // FFN_ROW_SPLIT
    u_next = up(0)
    for j in range(n):
        ua, ug = u_next
        if j + 1 < n:
            u_next = up(j + 1)
        ua_p, ua_n = pltpu.roll(ua, 1, 0), pltpu.roll(ua, rows - 1, 0)
        ug_p, ug_n = pltpu.roll(ug, 1, 0), pltpu.roll(ug, rows - 1, 0)
        for h in range(FFN_ROW_SPLIT):
            a = conv(ua, ua_p, ua_n, j * tf, h * nr, nr)
            half_gate = 0.5 * conv(ug, ug_p, ug_n, dff + j * tf, h * nr, nr)
            act_ref[h * nr:(h + 1) * nr, j * tf:(j + 1) * tf] = (
                half_gate * (1.0 + jnp.tanh(half_gate)) * a).astype(BF16)
    y_ref[0] = _rms(x1[HALO:HALO + tm] + _dot(act_ref[...], wd_ref[...]), lnw_ref[...])


def _mix_ffn(o_f, o_b, gg, o_att, x, on, wo, lnf, w_up, conv_w, conv_b, w_down, lnw):
    B, L, D = x.shape
    tm = FFN_ROW_TILE
    hb = tm // HALO
    last_halo = L // HALO - 1
    full = lambda a: pl.BlockSpec(a.shape, lambda b, i: (0,) * a.ndim, pipeline_mode=pl.Buffered(1))

    def halo_tile_halo(w):
        return [pl.BlockSpec((1, HALO, w), lambda b, i: (b, jnp.maximum(i * hb - 1, 0), 0)),
                pl.BlockSpec((1, tm, w), lambda b, i: (b, i, 0)),
                pl.BlockSpec((1, HALO, w), lambda b, i: (b, jnp.minimum((i + 1) * hb, last_halo), 0))]

    acts = [o_f, o_b, gg, o_att, x]
    weights = [on, wo, lnf, w_up, conv_w, conv_b, w_down, lnw]
    return pl.pallas_call(
        _mix_ffn_kernel,
        grid=(B, L // tm),
        in_specs=sum([halo_tile_halo(a.shape[2]) for a in acts], []) + [full(w) for w in weights],
        out_specs=pl.BlockSpec((1, tm, D), lambda b, i: (b, i, 0)),
        out_shape=jax.ShapeDtypeStruct((B, L, D), F32),
        scratch_shapes=[pltpu.VMEM((tm, w_down.shape[0]), BF16)],
        compiler_params=pltpu.CompilerParams(
            dimension_semantics=("parallel", "parallel"), vmem_limit_bytes=VMEM_LIMIT),
        name="mix_ffn",
    )(*sum([[a, a, a] for a in acts], []), *weights)


def _rope_tables(L):
    rows = L // GRID_W
    row = jnp.repeat(jnp.arange(rows), GRID_W).astype(F32)
    col = jnp.tile(jnp.arange(GRID_W), rows).astype(F32)
    half = ATT_HD // 2
    inv = 1.0 / (ROPE_THETA ** (jnp.arange(0, half, 2, dtype=F32) / half))
    ang_r = inv[:, None] * row
    ang_c = inv[:, None] * col
    cr, sr, cc, sc = jnp.cos(ang_r), jnp.sin(ang_r), jnp.cos(ang_c), jnp.sin(ang_c)
    return jnp.concatenate([cr, cr, cc, cc], axis=0), jnp.concatenate([-sr, sr, -sc, sc], axis=0)


def _prepare(ln_mix, w_in, w_gk_fwd, b_gk_fwd, w_gk_bwd, b_gk_bwd, gla_out_norm, q_norm, k_norm,
             w_out, ln_ffn, w_up, conv_w, conv_b, w_down, ln_final, layer):
    pts = [0]
    for s in IN_SIZES:
        pts.append(pts[-1] + s)
    w = w_in[layer]
    piece = lambda a, b: w[:, pts[a]:pts[b]]
    zeros = jnp.zeros((GLA_RANK, GLA_QK_WIDTH), F32)
    sub = ROW_TILE // IN_PROJ_SUBTILES
    wgate = jnp.concatenate([jnp.concatenate([w_gk_fwd[layer], zeros], axis=1),
                             jnp.concatenate([zeros, w_gk_bwd[layer]], axis=1)], axis=0)
    return dict(
        lnw=ln_mix[layer][None, :],
        wg=piece(0, 4).astype(BF16),
        wa=piece(6, 9).astype(BF16),
        wr=piece(4, 6).astype(BF16),
        wgate=wgate.astype(BF16),
        bgate=jnp.concatenate([b_gk_fwd[layer], b_gk_bwd[layer]])[None, :],
        qn=jnp.broadcast_to(q_norm[layer][:, None], (ATT_HD, sub)),
        kn=jnp.broadcast_to(k_norm[layer][:, None], (ATT_HD, sub)),
        on=gla_out_norm[layer][None, :],
        wo=w_out[layer].astype(BF16),
        lnf=ln_ffn[layer][None, :],
        w_up=w_up[layer].astype(BF16),
        conv_w=conv_w[layer],
        conv_b=conv_b[layer][None, :],
        w_down=w_down[layer].astype(BF16),
    )


def _trunk(x, p, cos, sin, ln_final):
    gq, gk, gv, gg, gf, gb, qT, k, vT = _in_proj(
        x, p["lnw"], p["wg"], p["wa"], p["wr"], p["wgate"], p["bgate"], p["qn"], p["kn"], cos, sin)
    o_f, o_b = _gla(gq, gk, gv, gf, gb)
    o_att = _attention(qT, k, vT)
    return _mix_ffn(o_f, o_b, gg, o_att, x, p["on"], p["wo"], p["lnf"],
                    p["w_up"], p["conv_w"], p["conv_b"], p["w_down"], ln_final[None, :])


def kernel(x_prompt, x_sample, ln_mix, w_in, w_gk_fwd, b_gk_fwd, w_gk_bwd, b_gk_bwd, gla_out_norm,
           q_norm, k_norm, w_out, ln_ffn, w_up, conv_w, conv_b, w_down, ln_final):
    assert w_in.shape[0] == 1, "the mix_ffn kernel fuses the final RMSNorm: single-layer trunk only"
    p = _prepare(ln_mix, w_in, w_gk_fwd, b_gk_fwd, w_gk_bwd, b_gk_bwd, gla_out_norm, q_norm,
                 k_norm, w_out, ln_ffn, w_up, conv_w, conv_b, w_down, ln_final, 0)
    outs = []
    for x in (x_prompt, x_sample):
        cos, sin = _rope_tables(x.shape[1])
        outs.append(_trunk(x, p, cos, sin, ln_final))
    return tuple(outs)
```
